```python
import math
import jax, jax.numpy as jnp
from jax import lax
import numpy as np

D_MODEL = 4096
BATCH = 4
SEQ = 2048
DEPTH = 2
DEC_BATCH = 8
DEC_SEQ = 4
PAST_LEN = 16384
PAGE_SIZE = 128

D_RNN = D_MODEL
LRU_BLOCKS = 16
LRU_BLOCK = D_RNN // LRU_BLOCKS
CONV_W = 4
LRU_C = 8.0
HEAD_DIM = 128
HEADS_PER_GROUP = 8
ATT_GROUPS = ((128, 1), (512, 4), (2048, 16))
ATT_HEADS = len(ATT_GROUPS) * HEADS_PER_GROUP
QKV_W = ATT_HEADS * HEAD_DIM
GROUP_WIDTH = HEADS_PER_GROUP * HEAD_DIM
ATT_SCALE = HEAD_DIM ** -0.5
REL_BUCKETS = 32
REL_MAX_DIST = 2048
N_EXPERTS = 64
N_EXPERT_GROUPS = 8
TOPK_GROUPS = 4
TOP_K = 8
D_EXPERT = 1024
D_SHARED = 1024
ROUTED_SCALE = 2.5
MOE_BLOCK = 128
LN_EPS = 1e-5
DN_ALPHA = (2 * DEPTH) ** 0.25
DN_BETA = (8 * DEPTH) ** -0.25
IN_SPLITS = (D_RNN, D_RNN + QKV_W, D_RNN + 2 * QKV_W, D_RNN + 3 * QKV_W, D_RNN + 3 * QKV_W + D_MODEL)
IN_COLS = D_RNN + 3 * QKV_W + 2 * D_MODEL

kernel_name = 'hawk_dilated_attn_moe_decoder_step'


def layer_norm(x, g, b):
    xf = x.astype(jnp.float32)
    mu = jnp.mean(xf, axis=-1, keepdims=True)
    var = jnp.mean(jnp.square(xf - mu), axis=-1, keepdims=True)
    y = (xf - mu) * lax.rsqrt(var + LN_EPS) * g.astype(jnp.float32) + b.astype(jnp.float32)
    return y.astype(x.dtype)


def rel_bucket(dist):
    max_exact = REL_BUCKETS // 2
    df = jnp.maximum(dist, 1).astype(jnp.float32)
    large = max_exact + (jnp.log(df / max_exact) / math.log(REL_MAX_DIST / max_exact)
                         * (REL_BUCKETS - max_exact)).astype(jnp.int32)
    large = jnp.minimum(large, REL_BUCKETS - 1)
    return jnp.where(dist < max_exact, dist, large)


def causal_conv(x, prev, w, b):
    T = x.shape[1]
    xp = jnp.concatenate([prev.astype(x.dtype), x], axis=1)
    y = b + xp[:, 0:T] * w[0]
    for j in range(1, CONV_W):
        y = y + xp[:, j:j + T] * w[j]
    return y, xp[:, T:]


def rglru(x, h0, w_a, b_a, w_x, b_x, lam):
    Bn, T, _ = x.shape
    xb = x.reshape(Bn, T, LRU_BLOCKS, LRU_BLOCK)
    r = jax.nn.sigmoid(jnp.einsum('btni,nij->btnj', xb, w_a).reshape(Bn, T, D_RNN) + b_a)
    gi = jax.nn.sigmoid(jnp.einsum('btni,nij->btnj', xb, w_x).reshape(Bn, T, D_RNN) + b_x)
    log_a = LRU_C * r.astype(jnp.float32) * jax.nn.log_sigmoid(lam.astype(jnp.float32))
    a = jnp.exp(log_a)
    bt = jnp.sqrt(-jnp.expm1(2.0 * log_a)) * (gi * x).astype(jnp.float32)
    bt = bt.at[:, 0].add(a[:, 0] * h0.astype(jnp.float32))

    def combine(left, right):
        a_l, b_l = left
        a_r, b_r = right
        return a_l * a_r, a_r * b_l + b_r

    _, h = lax.associative_scan(combine, (a, bt), axis=1)
    return h.astype(x.dtype), h[:, -1]


def dilated_attn_prompt(q, k, v, table_g, window, dil):
    Bn, S, H, E = q.shape
    band = window // dil
    L = S // dil
    nb = -(-L // band)
    Lp = nb * band

    def to_blocks(t):
        t = t.reshape(Bn, L, dil, H, E).transpose(0, 2, 1, 3, 4)
        t = jnp.pad(t, ((0, 0), (0, 0), (0, Lp - L), (0, 0), (0, 0)))
        return t.reshape(Bn, dil, nb, band, H, E)

    def with_prev(t):
        prev = jnp.pad(t, ((0, 0), (0, 0), (1, 0), (0, 0), (0, 0), (0, 0)))[:, :, :-1]
        return jnp.concatenate([prev, t], axis=3)

    qb = to_blocks(q)
    kk = with_prev(to_blocks(k))
    vv = with_prev(to_blocks(v))
    qi = jnp.arange(band)[:, None]
    kj = jnp.arange(2 * band)[None, :]
    rel = qi + band - kj
    in_band = (rel >= 0) & (rel <= band)
    bias = table_g[rel_bucket(jnp.clip(rel, 0, band) * dil)].transpose(2, 0, 1)
    exists = (jnp.arange(nb)[:, None] > 0) | (kj >= band)
    mask = in_band[None] & exists[:, None, :]
    logits = jnp.einsum('brnqhe,brnkhe->brnhqk', qb, kk,
                        preferred_element_type=jnp.float32) * ATT_SCALE + bias.astype(jnp.float32)
    logits = jnp.where(mask[None, None, :, None], logits, -jnp.inf)
    mx = jnp.max(logits, axis=-1, keepdims=True)
    p = jnp.exp(logits - mx)
    den = jnp.sum(p, axis=-1, keepdims=True)
    o = jnp.einsum('brnhqk,brnkhe->brnqhe', p / den, vv.astype(jnp.float32))
    lse = (mx + jnp.log(den))[..., 0]
    o = o.reshape(Bn, dil, Lp, H, E)[:, :, :L].transpose(0, 2, 1, 3, 4).reshape(Bn, S, H, E)
    lse = lse.transpose(0, 1, 2, 4, 3).reshape(Bn, dil, Lp, H)[:, :, :L]
    lse = lse.transpose(0, 2, 1, 3).reshape(Bn, S, H)
    return o, lse


def dilated_attn_sample(q, k, v, kv_buf, table_g, window, dil):
    Bn, T, H, E = q.shape
    Lb = kv_buf.shape[2]
    nk = window // dil + 1
    kc = jnp.concatenate([kv_buf[:, 0].astype(k.dtype), k], axis=1)
    vc = jnp.concatenate([kv_buf[:, 1].astype(v.dtype), v], axis=1)
    m_off = jnp.arange(nk)
    idx = Lb + jnp.arange(T)[:, None] - m_off[None, :] * dil
    valid = idx >= 0
    idx = jnp.maximum(idx, 0)
    kg = kc[:, idx]
    vg = vc[:, idx]
    bias = table_g[rel_bucket(m_off * dil)].T
    logits = jnp.einsum('bthe,btmhe->bthm', q, kg,
                        preferred_element_type=jnp.float32) * ATT_SCALE + bias.astype(jnp.float32)
    logits = jnp.where(valid[None, :, None, :], logits, -jnp.inf)
    mx = jnp.max(logits, axis=-1, keepdims=True)
    p = jnp.exp(logits - mx)
    den = jnp.sum(p, axis=-1, keepdims=True)
    o = jnp.einsum('bthm,btmhe->bthe', p / den, vg.astype(jnp.float32))
    lse = (mx + jnp.log(den))[..., 0]
    return o, lse


def combine_groups(outs, lses):
    wts = jax.nn.softmax(jnp.stack(lses, axis=0), axis=0)
    return jnp.einsum('gbth,gbthe->bthe', wts, jnp.stack(outs, axis=0))


def token_mixer(h, conv_prev, h_prev, kv_bufs, rel_bias, w_in, conv_w, conv_b, lru_wa, lru_ba,
                lru_wx, lru_bx, lru_lam, w_pa, w_pb, w_o):
    Bn, T, _ = h.shape
    z = h @ w_in
    xr, q, k, v, ga, gb = jnp.split(z, IN_SPLITS, axis=-1)
    q = q.reshape(Bn, T, ATT_HEADS, HEAD_DIM)
    k = k.reshape(Bn, T, ATT_HEADS, HEAD_DIM)
    v = v.reshape(Bn, T, ATT_HEADS, HEAD_DIM)
    xc, conv_new = causal_conv(xr, conv_prev, conv_w, conv_b)
    hr, h_last = rglru(xc, h_prev, lru_wa, lru_ba, lru_wx, lru_bx, lru_lam)
    outs, lses, kv_new = [], [], []
    for g, (win, dil) in enumerate(ATT_GROUPS):
        sl = slice(g * HEADS_PER_GROUP, (g + 1) * HEADS_PER_GROUP)
        qg, kg, vg = q[:, :, sl], k[:, :, sl], v[:, :, sl]
        tg = rel_bias[:, sl]
        if kv_bufs is None:
            o, l = dilated_attn_prompt(qg, kg, vg, tg, win, dil)
            keep = min(win, T)
            kv_new.append(jnp.stack([kg[:, T - keep:], vg[:, T - keep:]], axis=1))
        else:
            o, l = dilated_attn_sample(qg, kg, vg, kv_bufs[g], tg, win, dil)
            kv_new.append(jnp.stack([kg, vg], axis=1))
        outs.append(o)
        lses.append(l)
    att = combine_groups(outs, lses).reshape(Bn, T, GROUP_WIDTH).astype(h.dtype)
    merged = jax.nn.sigmoid(ga) * (hr @ w_pa) + jax.nn.sigmoid(gb) * (att @ w_pb)
    return merged @ w_o, conv_new, h_last.astype(h_prev.dtype), kv_new


def swiglu(x, w1, w3, w2):
    return (jax.nn.silu(x @ w1) * (x @ w3)) @ w2


def routed_experts(x2, topi, topw, w1, w3, w2):
    N, D = x2.shape
    M = N * TOP_K
    blk = min(MOE_BLOCK, max(8, M // N_EXPERTS))
    n_blocks = -(-M // blk) + N_EXPERTS
    e_flat = topi.reshape(M)
    order = jnp.argsort(e_flat)
    e_sorted = e_flat[order]
    counts = jnp.zeros((N_EXPERTS,), jnp.int32).at[e_flat].add(1)
    starts = jnp.cumsum(counts) - counts
    padded = (counts + blk - 1) // blk * blk
    ends_p = jnp.cumsum(padded)
    dest = (ends_p - padded)[e_sorted] + jnp.arange(M, dtype=jnp.int32) - starts[e_sorted]
    slot_tok = jnp.zeros((n_blocks * blk,), jnp.int32).at[dest].set((order // TOP_K).astype(jnp.int32))
    slot_w = jnp.zeros((n_blocks * blk,), topw.dtype).at[dest].set(topw.reshape(M)[order])
    block_e = jnp.minimum(jnp.searchsorted(ends_p, jnp.arange(n_blocks, dtype=jnp.int32) * blk,
                                           side='right'), N_EXPERTS - 1)

    def block_ffn(args):
        tok, wt, e = args
        return swiglu(x2[tok], w1[e], w3[e], w2[e]) * wt[:, None].astype(x2.dtype)

    out = lax.map(block_ffn, (slot_tok.reshape(n_blocks, blk), slot_w.reshape(n_blocks, blk), block_e))
    return jnp.zeros_like(x2).at[slot_tok].add(out.reshape(n_blocks * blk, D))


def moe(h, w_router, b_router, w_e1, w_e3, w_e2, w_s1, w_s3, w_s2):
    Bn, T, D = h.shape
    N = Bn * T
    x2 = h.reshape(N, D)
    s = jax.nn.sigmoid(jnp.dot(x2, w_router, preferred_element_type=jnp.float32))
    sel = s + b_router.astype(jnp.float32)
    grp = jnp.sum(lax.top_k(sel.reshape(N, N_EXPERT_GROUPS, N_EXPERTS // N_EXPERT_GROUPS), 2)[0], axis=-1)
    _, gi = lax.top_k(grp, TOPK_GROUPS)
    gmask = jnp.sum(jax.nn.one_hot(gi, N_EXPERT_GROUPS), axis=1) > 0
    emask = jnp.repeat(gmask, N_EXPERTS // N_EXPERT_GROUPS, axis=1)
    _, topi = lax.top_k(jnp.where(emask, sel, -jnp.inf), TOP_K)
    topw = jnp.take_along_axis(s, topi, axis=1)
    topw = topw / jnp.sum(topw, axis=-1, keepdims=True) * ROUTED_SCALE
    y = routed_experts(x2, topi, topw, w_e1, w_e3, w_e2) + swiglu(x2, w_s1, w_s3, w_s2)
    return y.reshape(Bn, T, D)


def trunk_layer(x, c, conv_prev, h_prev, kv_bufs, rel_bias, norm_w, mix_w, ffn_w):
    w_mod, b_mod, ln1_g, ln1_b, ln2_g, ln2_b = norm_w
    mod = (jax.nn.silu(c) @ w_mod + b_mod)[:, None, :]
    sh1, sc1, g1, sh2, sc2, g2 = jnp.split(mod, 6, axis=-1)
    mix, conv_new, h_new, kv_new = token_mixer(x * (1.0 + sc1) + sh1, conv_prev, h_prev, kv_bufs,
                                               rel_bias, *mix_w)
    x = layer_norm(DN_ALPHA * x + g1 * mix, ln1_g, ln1_b)
    ffn = moe(x * (1.0 + sc2) + sh2, *ffn_w)
    x = layer_norm(DN_ALPHA * x + g2 * ffn, ln2_g, ln2_b)
    return x, conv_new, h_new, kv_new


def setup_inputs(seed: int = 0) -> dict:
    key = jax.random.key(seed)
    ks = iter(jax.random.split(key, 48))
    f32 = jnp.float32

    def nrm(shape, scale):
        return jax.random.normal(next(ks), shape, f32) * scale

    d_in = D_MODEL ** -0.5
    u = jax.random.uniform(next(ks), (DEPTH, D_RNN), f32, 0.9, 0.999)
    p_a = u ** (1.0 / LRU_C)
    return {
        'x_prompt': nrm((BATCH, SEQ, D_MODEL), 1.0),
        'x_sample': nrm((DEC_BATCH, DEC_SEQ, D_MODEL), 1.0),
        'c_prompt': nrm((BATCH, D_MODEL), 1.0),
        'c_sample': nrm((DEC_BATCH, D_MODEL), 1.0),
        'cache_kv_w128': nrm((DEC_BATCH, DEPTH, 2, min(ATT_GROUPS[0][0], PAST_LEN), HEADS_PER_GROUP, HEAD_DIM), 1.0),
        'cache_kv_w512': nrm((DEC_BATCH, DEPTH, 2, min(ATT_GROUPS[1][0], PAST_LEN), HEADS_PER_GROUP, HEAD_DIM), 1.0),
        'cache_kv_w2048': nrm((DEC_BATCH, DEPTH, 2, min(ATT_GROUPS[2][0], PAST_LEN), HEADS_PER_GROUP, HEAD_DIM), 1.0),
        'state_rglru_h': nrm((DEC_BATCH, DEPTH, D_RNN), 0.5),
        'state_conv': nrm((DEC_BATCH, DEPTH, CONV_W - 1, D_RNN), 1.0),
        'rel_bias': nrm((REL_BUCKETS, ATT_HEADS), 0.5),
        'w_mod': nrm((DEPTH, D_MODEL, 6 * D_MODEL), 0.5 * d_in),
        'b_mod': nrm((DEPTH, 6 * D_MODEL), 0.02),
        'w_in': nrm((DEPTH, D_MODEL, IN_COLS), d_in),
        'conv_w': nrm((DEPTH, CONV_W, D_RNN), CONV_W ** -0.5),
        'conv_b': nrm((DEPTH, D_RNN), 0.02),
        'lru_wa': nrm((DEPTH, LRU_BLOCKS, LRU_BLOCK, LRU_BLOCK), LRU_BLOCK ** -0.5),
        'lru_ba': nrm((DEPTH, D_RNN), 0.02),
        'lru_wx': nrm((DEPTH, LRU_BLOCKS, LRU_BLOCK, LRU_BLOCK), LRU_BLOCK ** -0.5),
        'lru_bx': nrm((DEPTH, D_RNN), 0.02),
        'lru_lam': jnp.log(p_a) - jnp.log1p(-p_a),
        'w_pa': nrm((DEPTH, D_RNN, D_MODEL), D_RNN ** -0.5),
        'w_pb': nrm((DEPTH, GROUP_WIDTH, D_MODEL), GROUP_WIDTH ** -0.5),
        'w_o': nrm((DEPTH, D_MODEL, D_MODEL), DN_BETA * d_in),
        'ln1_g': 1.0 + nrm((DEPTH, D_MODEL), 0.02),
        'ln1_b': nrm((DEPTH, D_MODEL), 0.02),
        'w_router': nrm((DEPTH, D_MODEL, N_EXPERTS), d_in),
        'b_router': nrm((DEPTH, N_EXPERTS), 0.01),
        'w_e1': nrm((DEPTH, N_EXPERTS, D_MODEL, D_EXPERT), d_in),
        'w_e3': nrm((DEPTH, N_EXPERTS, D_MODEL, D_EXPERT), d_in),
        'w_e2': nrm((DEPTH, N_EXPERTS, D_EXPERT, D_MODEL), DN_BETA * D_EXPERT ** -0.5),
        'w_s1': nrm((DEPTH, D_MODEL, D_SHARED), d_in),
        'w_s3': nrm((DEPTH, D_MODEL, D_SHARED), d_in),
        'w_s2': nrm((DEPTH, D_SHARED, D_MODEL), DN_BETA * D_SHARED ** -0.5),
        'ln2_g': 1.0 + nrm((DEPTH, D_MODEL), 0.02),
        'ln2_b': nrm((DEPTH, D_MODEL), 0.02),
    }


def reference(x_prompt, x_sample, c_prompt, c_sample, cache_kv_w128, cache_kv_w512, cache_kv_w2048,
              state_rglru_h, state_conv, rel_bias, w_mod, b_mod, w_in, conv_w, conv_b, lru_wa, lru_ba,
              lru_wx, lru_bx, lru_lam, w_pa, w_pb, w_o, ln1_g, ln1_b, w_router, b_router, w_e1, w_e3,
              w_e2, w_s1, w_s3, w_s2, ln2_g, ln2_b):
    kv_caches = (cache_kv_w128, cache_kv_w512, cache_kv_w2048)
    n_p = x_prompt.shape[0]
    xp, xs = x_prompt, x_sample
    p_kv, p_h, p_conv, s_kv, s_h, s_conv = [], [], [], [], [], []
    for l in range(DEPTH):
        norm_w = (w_mod[l], b_mod[l], ln1_g[l], ln1_b[l], ln2_g[l], ln2_b[l])
        mix_w = (w_in[l], conv_w[l], conv_b[l], lru_wa[l], lru_ba[l], lru_wx[l], lru_bx[l], lru_lam[l],
                 w_pa[l], w_pb[l], w_o[l])
        ffn_w = (w_router[l], b_router[l], w_e1[l], w_e3[l], w_e2[l], w_s1[l], w_s3[l], w_s2[l])
        xp, conv_n, h_n, kv_n = trunk_layer(
            xp, c_prompt, jnp.zeros((n_p, CONV_W - 1, D_RNN), x_prompt.dtype),
            jnp.zeros((n_p, D_RNN), x_prompt.dtype), None, rel_bias, norm_w, mix_w, ffn_w)
        p_kv.append(kv_n)
        p_h.append(h_n)
        p_conv.append(conv_n)
        xs, conv_n, h_n, kv_n = trunk_layer(
            xs, c_sample, state_conv[:, l], state_rglru_h[:, l], [kc[:, l] for kc in kv_caches],
            rel_bias, norm_w, mix_w, ffn_w)
        s_kv.append(kv_n)
        s_h.append(h_n)
        s_conv.append(conv_n)
    new_kv_w128_prompt = jnp.stack([r[0] for r in p_kv], axis=1)
    new_kv_w512_prompt = jnp.stack([r[1] for r in p_kv], axis=1)
    new_kv_w2048_prompt = jnp.stack([r[2] for r in p_kv], axis=1)
    new_h_prompt = jnp.stack(p_h, axis=1)
    new_conv_prompt = jnp.stack(p_conv, axis=1)
    new_kv_w128_sample = jnp.stack([r[0] for r in s_kv], axis=1)
    new_kv_w512_sample = jnp.stack([r[1] for r in s_kv], axis=1)
    new_kv_w2048_sample = jnp.stack([r[2] for r in s_kv], axis=1)
    new_h_sample = jnp.stack(s_h, axis=1)
    new_conv_sample = jnp.stack(s_conv, axis=1)
    return (xp, xs, new_kv_w128_prompt, new_kv_w512_prompt, new_kv_w2048_prompt, new_h_prompt,
            new_conv_prompt, new_kv_w128_sample, new_kv_w512_sample, new_kv_w2048_sample,
            new_h_sample, new_conv_sample)
```

```python
import functools
import math

import numpy as np
import jax
import jax.numpy as jnp
from jax import lax
from jax.experimental import pallas as pl
from jax.experimental.pallas import tpu as pltpu

bf16 = jnp.bfloat16
f32 = jnp.float32

D_MODEL = 4096
DEPTH = 2
LRU_BLOCKS = 16
CONV_W = 4
LRU_C = 8.0
HEAD_DIM = 128
HEADS_PER_GROUP = 8
ATT_GROUPS = ((128, 1), (512, 4), (2048, 16))
REL_BUCKETS = 32
REL_MAX_DIST = 2048
N_EXPERTS = 64
N_EXPERT_GROUPS = 8
TOPK_GROUPS = 4
TOP_K = 8
ROUTED_SCALE = 2.5
LN_EPS = 1e-5

V7X_VMEM_LIMIT_BYTES = 56 * 1024 * 1024
LANE = 128
SUBLANE = 8
SAMPLE_ROWS = 8
TM_MATMUL = 1024
TN_MATMUL = 512
TM_MERGE = 512
TM_TOKEN = 256
T_SCAN = 256
TM_EXPERT = 512
CH_EXPERT = 256
TN_EXPERT = 1024
T_GATHER = 128
T_COMBINE = 128


def _derived():
    d_rnn = D_MODEL
    gw = HEADS_PER_GROUP * HEAD_DIM
    qkv = len(ATT_GROUPS) * gw
    return dict(d_rnn=d_rnn, gw=gw, qkv=qkv, in_cols=d_rnn + 3 * qkv + 2 * D_MODEL,
                dn_alpha=(2 * DEPTH) ** 0.25, att_scale=HEAD_DIM ** -0.5,
                lru_block=d_rnn // LRU_BLOCKS)


def _cparams(*sem):
    return pltpu.CompilerParams(dimension_semantics=sem, vmem_limit_bytes=V7X_VMEM_LIMIT_BYTES)


def _tile(n, pref):
    return pref if n % pref == 0 else n


def _nt_dot(a, b):
    return lax.dot_general(a, b, (((1,), (1,)), ((), ())), preferred_element_type=f32)


def _mm_body(*refs, silu_a, has_bias):
    if has_bias:
        a_ref, b_ref, bias_ref, o_ref = refs
    else:
        a_ref, b_ref, o_ref = refs
    a = a_ref[...]
    if silu_a:
        a = a * jax.nn.sigmoid(a)
    acc = jnp.dot(a.astype(bf16), b_ref[...].astype(bf16), preferred_element_type=f32)
    if has_bias:
        acc = acc + bias_ref[...]
    o_ref[...] = acc.astype(o_ref.dtype)


def _matmul(a, w, l, *, name, bias=None, silu_a=False, out_dtype=f32):
    m, k = a.shape
    n = w.shape[-1]
    tm, tn = _tile(m, TM_MATMUL), _tile(n, TN_MATMUL)
    in_specs = [pl.BlockSpec((tm, k), lambda i, j: (i, 0)),
                pl.BlockSpec((None, k, tn), lambda i, j: (l, 0, j))]
    args = [a, w]
    if bias is not None:
        in_specs.append(pl.BlockSpec((None, 1, tn), lambda i, j: (l, 0, j)))
        args.append(bias.reshape(bias.shape[0], 1, n))
    return pl.pallas_call(
        functools.partial(_mm_body, silu_a=silu_a, has_bias=bias is not None),
        grid=(m // tm, n // tn), in_specs=in_specs,
        out_specs=pl.BlockSpec((tm, tn), lambda i, j: (i, j)),
        out_shape=jax.ShapeDtypeStruct((m, n), out_dtype),
        compiler_params=_cparams("parallel", "arbitrary"), name=name)(*args)


class _Group:
    def __init__(self, rows, tm, mod, per_row, rows_per_seq):
        self.rows, self.tm, self.mod, self.per_row, self.rows_per_seq = rows, tm, mod, per_row, rows_per_seq

    def mod_spec(self, which):
        d = D_MODEL
        if self.per_row:
            return pl.BlockSpec((self.tm, d), lambda i: (i, which))
        tiles = self.rows_per_seq // self.tm
        return pl.BlockSpec((None, 1, d), lambda i: (i // tiles, 0, which))

    def row_spec(self, width):
        return pl.BlockSpec((self.tm, width), lambda i: (i, 0))


def _param_spec(l, width):
    return pl.BlockSpec((None, 1, width), lambda i: (l, 0, 0))


def _modulate_body(x_ref, sc_ref, sh_ref, o_ref):
    o_ref[...] = (x_ref[...] * (1.0 + sc_ref[...]) + sh_ref[...]).astype(o_ref.dtype)


def _modulate(x, grp, *, name):
    d = D_MODEL
    return pl.pallas_call(
        _modulate_body, grid=(grp.rows // grp.tm,),
        in_specs=[grp.row_spec(d), grp.mod_spec(1), grp.mod_spec(0)],
        out_specs=grp.row_spec(d), out_shape=jax.ShapeDtypeStruct((grp.rows, d), bf16),
        compiler_params=_cparams("parallel"), name=name)(x, grp.mod, grp.mod)


def _log_sigmoid(x):
    return -(jnp.maximum(-x, 0.0) + jnp.log1p(jnp.exp(-jnp.abs(x))))


def _neg_expm1(y):
    t = jnp.tanh(0.5 * y)
    return -2.0 * t / (1.0 - t)


def _rglru_body(x_ref, cp_ref, h0_ref, cw_ref, cb_ref, wa_ref, ba_ref, wx_ref, bx_ref, lam_ref,
                hr_ref, hl_ref, prev_scr, h_scr, *, tt, t_last):
    i = pl.program_id(2)

    @pl.when(i == 0)
    def _():
        if tt > SUBLANE:
            prev_scr[0:tt - SUBLANE, :] = jnp.zeros((tt - SUBLANE, prev_scr.shape[1]), f32)
        prev_scr[tt - SUBLANE:tt, :] = cp_ref[...]
        h_scr[...] = h0_ref[...]

    x = x_ref[...]
    prev = prev_scr[...]
    row = lax.broadcasted_iota(jnp.int32, x.shape, 0)

    def shifted(k):
        return jnp.where(row < k, pltpu.roll(prev, k, 0), pltpu.roll(x, k, 0))

    cw = cw_ref[...]
    xc = cb_ref[...] + shifted(3) * cw[0:1]
    xc = xc + shifted(2) * cw[1:2]
    xc = xc + shifted(1) * cw[2:3]
    xc = xc + x * cw[3:4]
    prev_scr[...] = x

    xcb = xc.astype(bf16)
    r = jax.nn.sigmoid(jnp.dot(xcb, wa_ref[...].astype(bf16), preferred_element_type=f32) + ba_ref[...])
    gi = jax.nn.sigmoid(jnp.dot(xcb, wx_ref[...].astype(bf16), preferred_element_type=f32) + bx_ref[...])
    log_a = LRU_C * r * _log_sigmoid(lam_ref[...])
    a = jnp.exp(log_a)
    b = jnp.sqrt(_neg_expm1(2.0 * log_a)) * (gi * xc)

    s = 1
    while s < tt:
        a_sh = jnp.where(row < s, 1.0, pltpu.roll(a, s, 0))
        b_sh = jnp.where(row < s, 0.0, pltpu.roll(b, s, 0))
        b = a * b_sh + b
        a = a * a_sh
        s *= 2
    h = a * h_scr[...] + b
    hr_ref[...] = h.astype(hr_ref.dtype)
    h_scr[...] = h[tt - 1:tt, :]
    hl_ref[...] = h[t_last:t_last + 1, :]


def _rglru(z, conv_prev8, h0, p, l, *, n_seq, t_seq, t_last, name):
    dd = _derived()
    lb = dd["lru_block"]
    tt = _tile(t_seq, T_SCAN)
    n_t = t_seq // tt
    d_rnn = dd["d_rnn"]

    def vec(arr):
        return arr.reshape(DEPTH, 1, d_rnn)

    vspec = pl.BlockSpec((None, 1, lb), lambda b, n, i: (l, 0, n))
    wspec = pl.BlockSpec((None, None, lb, lb), lambda b, n, i: (l, n, 0, 0))
    hr, h_last = pl.pallas_call(
        functools.partial(_rglru_body, tt=tt, t_last=t_last),
        grid=(n_seq, LRU_BLOCKS, n_t),
        in_specs=[pl.BlockSpec((tt, lb), lambda b, n, i: (b * n_t + i, n)),
                  pl.BlockSpec((None, SUBLANE, lb), lambda b, n, i: (b, 0, n)),
                  pl.BlockSpec((None, 1, lb), lambda b, n, i: (b, 0, n)),
                  pl.BlockSpec((None, CONV_W, lb), lambda b, n, i: (l, 0, n)),
                  vspec, wspec, vspec, wspec, vspec, vspec],
        out_specs=[pl.BlockSpec((tt, lb), lambda b, n, i: (b * n_t + i, n)),
                   pl.BlockSpec((None, 1, lb), lambda b, n, i: (b, 0, n))],
        out_shape=[jax.ShapeDtypeStruct((n_seq * t_seq, d_rnn), bf16),
                   jax.ShapeDtypeStruct((n_seq, 1, d_rnn), f32)],
        scratch_shapes=[pltpu.VMEM((tt, lb), f32), pltpu.VMEM((1, lb), f32)],
        compiler_params=_cparams("parallel", "parallel", "arbitrary"), name=name,
    )(z, conv_prev8, h0.reshape(n_seq, 1, d_rnn), p["conv_w"], vec(p["conv_b"]), p["lru_wa"],
      vec(p["lru_ba"]), p["lru_wx"], vec(p["lru_bx"]), vec(p["lru_lam"]))
    return hr, h_last.reshape(n_seq, d_rnn)


def _rel_bucket(dist):
    max_exact = REL_BUCKETS // 2
    df = jnp.maximum(dist, 1).astype(f32)
    large = max_exact + (jnp.log(df / max_exact) / math.log(REL_MAX_DIST / max_exact)
                         * (REL_BUCKETS - max_exact)).astype(jnp.int32)
    large = jnp.minimum(large, REL_BUCKETS - 1)
    return jnp.where(dist < max_exact, dist, large)


def _prompt_bias(rel_bias, g, band, dil):
    rel = np.arange(band)[:, None] + band - np.arange(2 * band)[None, :]
    in_band = (rel >= 0) & (rel <= band)
    tab = rel_bias[:, g * HEADS_PER_GROUP:(g + 1) * HEADS_PER_GROUP]
    vals = tab[_rel_bucket(jnp.asarray(np.clip(rel, 0, band) * dil, jnp.int32))]
    vals = jnp.where(jnp.asarray(in_band)[:, :, None], vals.astype(f32), -jnp.inf)
    return vals.transpose(2, 0, 1)


def _sample_bias(rel_bias, g, win, dil, t_new, lb_cache):
    nk = win // dil + 1
    t = np.arange(SAMPLE_ROWS)[:, None]
    tab = rel_bias[:, g * HEADS_PER_GROUP:(g + 1) * HEADS_PER_GROUP]

    def table(delta):
        ok = (delta >= 0) & (delta % dil == 0) & (delta // dil < nk)
        vals = tab[_rel_bucket(jnp.asarray(np.maximum(delta, 0), jnp.int32))].astype(f32)
        vals = jnp.where(jnp.asarray(ok)[:, :, None], vals, -jnp.inf)
        vals = jnp.where(jnp.asarray(t < t_new)[:, :, None], vals, 0.0)
        return vals.transpose(2, 0, 1)

    cache = table(lb_cache + t - np.arange(lb_cache)[None, :])
    j = np.arange(SAMPLE_ROWS)[None, :]
    new = table(np.where(j < t_new, t - j, -1))
    return cache, new


def _attn_prompt_body(*refs, s_len, scale):
    n_g = len(ATT_GROUPS)
    o_ref, o_scr, l_scr = refs[4 * n_g:]
    for g, (win, dil) in enumerate(ATT_GROUPS):
        q_ref, k_ref, v_ref, b_ref = refs[4 * g:4 * g + 4]
        band = win // dil
        nb = (s_len // dil) // band
        bias = b_ref[...]
        for r in range(dil):
            for blk in range(nb):
                def rows(first_blk, n_rows):
                    start = r + first_blk * band * dil
                    return pl.ds(start, n_rows) if dil == 1 else pl.ds(start, n_rows, stride=dil)
                qrows = rows(blk, band)
                krows, bb = (qrows, bias[:, band:]) if blk == 0 else (rows(blk - 1, 2 * band), bias)
                q = q_ref[qrows, :].astype(bf16)
                k = k_ref[krows, :].astype(bf16)
                v = v_ref[krows, :].astype(bf16)
                s = _nt_dot(q, k) * scale + bb
                mx = jnp.max(s, axis=-1, keepdims=True)
                p = jnp.exp(s - mx)
                den = jnp.sum(p, axis=-1, keepdims=True)
                o = jnp.dot((p / den).astype(bf16), v, preferred_element_type=f32)
                o_scr[g, qrows, :] = o
                l_scr[g, qrows, :] = jnp.broadcast_to(mx + jnp.log(den), o.shape)
    ch = 256
    for c in range(s_len // ch):
        sl = pl.ds(c * ch, ch)
        ls = [l_scr[g, sl, :] for g in range(n_g)]
        m = functools.reduce(jnp.maximum, ls)
        es = [jnp.exp(x - m) for x in ls]
        tot = functools.reduce(lambda a, b: a + b, es)
        out = functools.reduce(lambda a, b: a + b, [(es[g] / tot) * o_scr[g, sl, :] for g in range(n_g)])
        o_ref[sl, :] = out.astype(o_ref.dtype)


def _attn_prompt(z, rel_bias, *, n_seq, s_len, name):
    dd = _derived()
    gw, d_rnn, qkv = dd["gw"], dd["d_rnn"], dd["qkv"]
    in_specs, args = [], []
    for g, (win, dil) in enumerate(ATT_GROUPS):
        band = win // dil
        assert (s_len // dil) % band == 0
        for part in range(3):
            c0 = (d_rnn + part * qkv + g * gw) // HEAD_DIM
            in_specs.append(pl.BlockSpec((s_len, HEAD_DIM), lambda b, h, c0=c0: (b, c0 + h)))
            args.append(z)
        in_specs.append(pl.BlockSpec((None, band, 2 * band), lambda b, h: (h, 0, 0)))
        args.append(_prompt_bias(rel_bias, g, band, dil))
    n_g = len(ATT_GROUPS)
    return pl.pallas_call(
        functools.partial(_attn_prompt_body, s_len=s_len, scale=dd["att_scale"]),
        grid=(n_seq, HEADS_PER_GROUP), in_specs=in_specs,
        out_specs=pl.BlockSpec((s_len, HEAD_DIM), lambda b, h: (b, h)),
        out_shape=jax.ShapeDtypeStruct((n_seq * s_len, gw), bf16),
        scratch_shapes=[pltpu.VMEM((n_g, s_len, HEAD_DIM), f32), pltpu.VMEM((n_g, s_len, HEAD_DIM), f32)],
        compiler_params=_cparams("parallel", "parallel"), name=name)(*args)


def _attn_sample_body(*refs, scale):
    n_g = len(ATT_GROUPS)
    o_ref = refs[7 * n_g]
    outs, lses = [], []
    for g in range(n_g):
        q_ref, kn_ref, vn_ref, kc_ref, vc_ref, bc_ref, bn_ref = refs[7 * g:7 * g + 7]
        q = q_ref[...].astype(bf16)
        s_c = _nt_dot(q, kc_ref[...].astype(bf16)) * scale + bc_ref[...]
        s_n = _nt_dot(q, kn_ref[...].astype(bf16)) * scale + bn_ref[...]
        mx = jnp.maximum(jnp.max(s_c, axis=-1, keepdims=True), jnp.max(s_n, axis=-1, keepdims=True))
        p_c = jnp.exp(s_c - mx)
        p_n = jnp.exp(s_n - mx)
        den = jnp.sum(p_c, axis=-1, keepdims=True) + jnp.sum(p_n, axis=-1, keepdims=True)
        o = jnp.dot((p_c / den).astype(bf16), vc_ref[...].astype(bf16), preferred_element_type=f32)
        o = o + jnp.dot((p_n / den).astype(bf16), vn_ref[...].astype(bf16), preferred_element_type=f32)
        outs.append(o)
        lses.append(mx + jnp.log(den))
    m = functools.reduce(jnp.maximum, lses)
    es = [jnp.exp(x - m) for x in lses]
    tot = functools.reduce(lambda a, b: a + b, es)
    o_ref[...] = functools.reduce(lambda a, b: a + b, [(es[g] / tot) * outs[g] for g in range(n_g)])


def _attn_sample(z3, caches, rel_bias, l, *, n_seq, t_new, name):
    dd = _derived()
    gw, d_rnn, qkv = dd["gw"], dd["d_rnn"], dd["qkv"]
    in_specs, args = [], []
    for g, (win, dil) in enumerate(ATT_GROUPS):
        cache = caches[g]
        lb_cache = cache.shape[3]
        cache = cache.reshape(n_seq, DEPTH, 2, lb_cache, gw)
        for part in range(3):
            c0 = (d_rnn + part * qkv + g * gw) // HEAD_DIM
            in_specs.append(pl.BlockSpec((None, SAMPLE_ROWS, HEAD_DIM), lambda b, h, c0=c0: (b, 0, c0 + h)))
            args.append(z3)
        for kv in range(2):
            in_specs.append(pl.BlockSpec((None, None, None, lb_cache, HEAD_DIM),
                                         lambda b, h, kv=kv: (b, l, kv, 0, h)))
            args.append(cache)
        b_cache, b_new = _sample_bias(rel_bias, g, win, dil, t_new, lb_cache)
        in_specs.append(pl.BlockSpec((None, SAMPLE_ROWS, lb_cache), lambda b, h: (h, 0, 0)))
        in_specs.append(pl.BlockSpec((None, SAMPLE_ROWS, SAMPLE_ROWS), lambda b, h: (h, 0, 0)))
        args += [b_cache, b_new]
    out = pl.pallas_call(
        functools.partial(_attn_sample_body, scale=dd["att_scale"]),
        grid=(n_seq, HEADS_PER_GROUP), in_specs=in_specs,
        out_specs=pl.BlockSpec((None, SAMPLE_ROWS, HEAD_DIM), lambda b, h: (b, 0, h)),
        out_shape=jax.ShapeDtypeStruct((n_seq, SAMPLE_ROWS, gw), f32),
        compiler_params=_cparams("parallel", "parallel"), name=name)(*args)
    return out.reshape(n_seq * SAMPLE_ROWS, gw)


def _merge_body(hr_ref, att_ref, wa_ref, wb_ref, ga_ref, gb_ref, o_ref):
    pa = jnp.dot(hr_ref[...].astype(bf16), wa_ref[...].astype(bf16), preferred_element_type=f32)
    pb = jnp.dot(att_ref[...].astype(bf16), wb_ref[...].astype(bf16), preferred_element_type=f32)
    o_ref[...] = (jax.nn.sigmoid(ga_ref[...]) * pa + jax.nn.sigmoid(gb_ref[...]) * pb).astype(o_ref.dtype)


def _merge(hr, att, z, p, l, *, name):
    dd = _derived()
    m = hr.shape[0]
    d = D_MODEL
    tm, tn = _tile(m, TM_MERGE), _tile(d, TN_MATMUL)
    ga0 = (dd["d_rnn"] + 3 * dd["qkv"]) // tn
    gb0 = ga0 + d // tn
    return pl.pallas_call(
        _merge_body, grid=(m // tm, d // tn),
        in_specs=[pl.BlockSpec((tm, hr.shape[1]), lambda i, j: (i, 0)),
                  pl.BlockSpec((tm, att.shape[1]), lambda i, j: (i, 0)),
                  pl.BlockSpec((None, hr.shape[1], tn), lambda i, j: (l, 0, j)),
                  pl.BlockSpec((None, att.shape[1], tn), lambda i, j: (l, 0, j)),
                  pl.BlockSpec((tm, tn), lambda i, j: (i, ga0 + j)),
                  pl.BlockSpec((tm, tn), lambda i, j: (i, gb0 + j))],
        out_specs=pl.BlockSpec((tm, tn), lambda i, j: (i, j)),
        out_shape=jax.ShapeDtypeStruct((m, d), bf16),
        compiler_params=_cparams("parallel", "arbitrary"), name=name)(hr, att, p["w_pa"], p["w_pb"], z, z)


def _layer_norm(v, g, b):
    mu = jnp.mean(v, axis=-1, keepdims=True)
    var = jnp.mean(jnp.square(v - mu), axis=-1, keepdims=True)
    return (v - mu) * lax.rsqrt(var + LN_EPS) * g + b


def _pack_halves(h):
    half = h.shape[1] // 2
    lo = lax.bitcast_convert_type(h[:, :half].astype(bf16).astype(f32), jnp.uint32)
    hi = lax.bitcast_convert_type(h[:, half:].astype(bf16).astype(f32), jnp.uint32)
    return (lo >> 16) | (hi & jnp.uint32(0xFFFF0000))


def _unpack_halves(xu):
    lo = lax.bitcast_convert_type(xu << 16, f32).astype(bf16)
    hi = lax.bitcast_convert_type(xu & jnp.uint32(0xFFFF0000), f32).astype(bf16)
    return lo, hi


def _split_bf16(x):
    hi = x.astype(bf16)
    return hi, (x - hi.astype(f32)).astype(bf16)


def _first_max(vals, idx):
    m = functools.reduce(jnp.maximum, [jnp.max(v, axis=0, keepdims=True) for v in vals])
    big = float(N_EXPERTS)
    cand = [jnp.min(jnp.where(v == m, i, big), axis=0, keepdims=True) for v, i in zip(vals, idx)]
    return m, functools.reduce(jnp.minimum, cand)


def _ln_router_body(x_ref, mix_ref, g1_ref, sc_ref, sh_ref, lg_ref, lb_ref, wr_ref, br_ref, base_ref,
                    x1_ref, hp_ref, topi_ref, topw_ref, rank_ref, cnt_ref, *, alpha):
    i = pl.program_id(0)
    x1 = _layer_norm(alpha * x_ref[...] + g1_ref[...] * mix_ref[...], lg_ref[...], lb_ref[...])
    x1_ref[...] = x1
    h = x1 * (1.0 + sc_ref[...]) + sh_ref[...]
    hp_ref[...] = _pack_halves(h)

    h_hi, h_lo = _split_bf16(h)
    w_hi, w_lo = _split_bf16(wr_ref[...])
    logits = _nt_dot(w_hi, h_hi) + _nt_dot(w_hi, h_lo) + _nt_dot(w_lo, h_hi)
    s = jax.nn.sigmoid(logits)
    sel = s + br_ref[...]
    tm = s.shape[1]
    per = N_EXPERTS // N_EXPERT_GROUPS
    sub = lax.broadcasted_iota(jnp.int32, (per, tm), 0).astype(f32)
    sel_g = [sel[g * per:(g + 1) * per, :] for g in range(N_EXPERT_GROUPS)]
    s_g = [s[g * per:(g + 1) * per, :] for g in range(N_EXPERT_GROUPS)]
    idx_g = [sub + float(g * per) for g in range(N_EXPERT_GROUPS)]
    neg = -jnp.inf

    grp = []
    for g in range(N_EXPERT_GROUPS):
        m1, i1 = _first_max([sel_g[g]], [idx_g[g]])
        m2 = jnp.max(jnp.where(idx_g[g] == i1, neg, sel_g[g]), axis=0, keepdims=True)
        grp.append(m1 + m2)
    keep = [jnp.zeros((1, tm), jnp.bool_) for _ in range(N_EXPERT_GROUPS)]
    for _ in range(TOPK_GROUPS):
        m = functools.reduce(jnp.maximum, grp)
        found = jnp.zeros((1, tm), jnp.bool_)
        for g in range(N_EXPERT_GROUPS):
            hit = (grp[g] == m) & jnp.logical_not(found)
            found = found | hit
            keep[g] = keep[g] | hit
            grp[g] = jnp.where(hit, neg, grp[g])
    cand = [jnp.where(keep[g], sel_g[g], neg) for g in range(N_EXPERT_GROUPS)]

    @pl.when(i == 0)
    def _():
        cnt_ref[...] = base_ref[...]

    picked = [jnp.zeros((per, tm), f32) for _ in range(N_EXPERT_GROUPS)]
    top_i, top_w = [], []
    for _ in range(TOP_K):
        _, ik = _first_max(cand, idx_g)
        hits = [idx_g[g] == ik for g in range(N_EXPERT_GROUPS)]
        wk = functools.reduce(lambda a, b: a + b,
                              [jnp.sum(jnp.where(hits[g], s_g[g], 0.0), axis=0, keepdims=True)
                               for g in range(N_EXPERT_GROUPS)])
        for g in range(N_EXPERT_GROUPS):
            cand[g] = jnp.where(hits[g], neg, cand[g])
            picked[g] = jnp.where(hits[g], 1.0, picked[g])
        top_i.append(ik)
        top_w.append(wk)
    wsum = functools.reduce(lambda a, b: a + b, top_w)

    onehot = jnp.concatenate(picked, axis=0)
    r_i = lax.broadcasted_iota(jnp.int32, (tm, tm), 0)
    c_i = lax.broadcasted_iota(jnp.int32, (tm, tm), 1)
    before = (r_i < c_i).astype(bf16)
    prefix = jnp.dot(onehot.astype(bf16), before, preferred_element_type=f32) + cnt_ref[:, 0:1]
    cnt_ref[...] = cnt_ref[...] + jnp.sum(onehot, axis=1, keepdims=True)
    pre_g = [prefix[g * per:(g + 1) * per, :] for g in range(N_EXPERT_GROUPS)]
    for k in range(TOP_K):
        rk = functools.reduce(lambda a, b: a + b,
                              [jnp.sum(jnp.where(idx_g[g] == top_i[k], pre_g[g], 0.0), axis=0, keepdims=True)
                               for g in range(N_EXPERT_GROUPS)])
        topi_ref[k:k + 1, :] = top_i[k].astype(jnp.int32)
        topw_ref[k:k + 1, :] = top_w[k] / wsum * ROUTED_SCALE
        rank_ref[k:k + 1, :] = rk.astype(jnp.int32)


def _ln_router(x, mix, grp, p, l, base_counts, *, name):
    dd = _derived()
    d = D_MODEL
    rows, tm = grp.rows, grp.tm
    w_rt = jnp.swapaxes(p["w_router"], 1, 2)
    b_r = p["b_router"].reshape(DEPTH, N_EXPERTS, 1)
    tok = lambda dt: jax.ShapeDtypeStruct((TOP_K, rows), dt)
    tok_spec = pl.BlockSpec((TOP_K, tm), lambda i: (0, i))
    return pl.pallas_call(
        functools.partial(_ln_router_body, alpha=dd["dn_alpha"]),
        grid=(rows // tm,),
        in_specs=[grp.row_spec(d), grp.row_spec(d), grp.mod_spec(2), grp.mod_spec(4), grp.mod_spec(3),
                  _param_spec(l, d), _param_spec(l, d),
                  pl.BlockSpec((None, N_EXPERTS, d), lambda i: (l, 0, 0)),
                  pl.BlockSpec((None, N_EXPERTS, 1), lambda i: (l, 0, 0)),
                  pl.BlockSpec((N_EXPERTS, LANE), lambda i: (0, 0))],
        out_specs=[grp.row_spec(d), grp.row_spec(d // 2), tok_spec, tok_spec, tok_spec,
                   pl.BlockSpec((N_EXPERTS, LANE), lambda i: (0, 0))],
        out_shape=[jax.ShapeDtypeStruct((rows, d), f32), jax.ShapeDtypeStruct((rows, d // 2), jnp.uint32),
                   tok(jnp.int32), tok(f32), tok(jnp.int32),
                   jax.ShapeDtypeStruct((N_EXPERTS, LANE), f32)],
        compiler_params=_cparams("arbitrary"), name=name,
    )(x, mix, grp.mod, grp.mod, grp.mod, p["ln1_g"].reshape(DEPTH, 1, d), p["ln1_b"].reshape(DEPTH, 1, d),
      w_rt, b_r, base_counts)


def _gather_body(idx_ref, src_ref, o_ref, sem, *, tg):
    base = pl.program_id(0) * tg

    def row_copy(r, src_row):
        return pltpu.make_async_copy(src_ref.at[pl.ds(src_row, 1), :], o_ref.at[pl.ds(r, 1), :], sem)

    def start(r, carry):
        row_copy(r, idx_ref[base + r]).start()
        return carry

    def wait(r, carry):
        row_copy(r, 0).wait()
        return carry

    lax.fori_loop(0, tg, start, 0)
    lax.fori_loop(0, tg, wait, 0)


def _gather_rows(src, idx, *, name):
    r_out, width = idx.shape[0], src.shape[1]
    tg = _tile(r_out, T_GATHER)
    return pl.pallas_call(
        functools.partial(_gather_body, tg=tg),
        grid_spec=pltpu.PrefetchScalarGridSpec(
            num_scalar_prefetch=1, grid=(r_out // tg,),
            in_specs=[pl.BlockSpec(memory_space=pl.ANY)],
            out_specs=pl.BlockSpec((tg, width), lambda i, idx_ref: (i, 0)),
            scratch_shapes=[pltpu.SemaphoreType.DMA(())]),
        out_shape=jax.ShapeDtypeStruct((r_out, width), src.dtype),
        compiler_params=_cparams("arbitrary"), name=name)(idx, src)


def _expert_changed(be_ref, i):
    return (i == 0) | (be_ref[i] != be_ref[jnp.maximum(i - 1, 0)])


def _ffn_a_body(be_ref, nu_ref, x_ref, w1_ref, w3_ref, o_ref, w1_scr, w3_scr):
    i = pl.program_id(1)

    @pl.when(i < nu_ref[0])
    def _():
        @pl.when(_expert_changed(be_ref, i))
        def _():
            w1_scr[...] = w1_ref[...].astype(bf16)
            w3_scr[...] = w3_ref[...].astype(bf16)

        lo, hi = _unpack_halves(x_ref[...])
        half = lo.shape[1]
        h1 = (jnp.dot(lo, w1_scr[0:half, :], preferred_element_type=f32)
              + jnp.dot(hi, w1_scr[half:, :], preferred_element_type=f32))
        h3 = (jnp.dot(lo, w3_scr[0:half, :], preferred_element_type=f32)
              + jnp.dot(hi, w3_scr[half:, :], preferred_element_type=f32))
        o_ref[...] = (h1 * jax.nn.sigmoid(h1) * h3).astype(o_ref.dtype)

    @pl.when(i >= nu_ref[0])
    def _():
        o_ref[...] = jnp.zeros(o_ref.shape, o_ref.dtype)


def _ffn_b_body(be_ref, nu_ref, a_ref, w2_ref, rw_ref, o_ref, w2_scr):
    i = pl.program_id(1)

    @pl.when(i < nu_ref[0])
    def _():
        @pl.when(_expert_changed(be_ref, i))
        def _():
            w2_scr[...] = w2_ref[...].astype(bf16)

        o_ref[...] = jnp.dot(a_ref[...], w2_scr[...], preferred_element_type=f32) * rw_ref[...]

    @pl.when(i >= nu_ref[0])
    def _():
        o_ref[...] = jnp.zeros(o_ref.shape, o_ref.dtype)


def _expert_ffn(xp, w1, w3, w2, lead, block_e, n_used, row_w, *, tm, name):
    r_rows, half = xp.shape
    d = 2 * half
    d_hid = w1.shape[-1]
    n_blocks = r_rows // tm
    ch, tn = _tile(d_hid, CH_EXPERT), _tile(d, TN_EXPERT)
    nl = len(lead)

    def blk(i, nu_ref):
        return jnp.minimum(i, nu_ref[0] - 1)

    def w_spec(shape, col_axis):
        def index(c, i, be_ref, nu_ref):
            e = be_ref[blk(i, nu_ref)]
            return lead + ((e, 0, c) if col_axis == 2 else (e, c, 0))
        return pl.BlockSpec((None,) * (nl + 1) + shape, index)

    act = pl.pallas_call(
        _ffn_a_body,
        grid_spec=pltpu.PrefetchScalarGridSpec(
            num_scalar_prefetch=2, grid=(d_hid // ch, n_blocks),
            in_specs=[pl.BlockSpec((tm, half), lambda c, i, be, nu: (blk(i, nu), 0)),
                      w_spec((d, ch), 2), w_spec((d, ch), 2)],
            out_specs=pl.BlockSpec((tm, ch), lambda c, i, be, nu: (i, c)),
            scratch_shapes=[pltpu.VMEM((d, ch), bf16), pltpu.VMEM((d, ch), bf16)]),
        out_shape=jax.ShapeDtypeStruct((r_rows, d_hid), bf16),
        compiler_params=_cparams("arbitrary", "arbitrary"), name=name + "_up")(block_e, n_used, xp, w1, w3)
    return pl.pallas_call(
        _ffn_b_body,
        grid_spec=pltpu.PrefetchScalarGridSpec(
            num_scalar_prefetch=2, grid=(d // tn, n_blocks),
            in_specs=[pl.BlockSpec((tm, d_hid), lambda c, i, be, nu: (blk(i, nu), 0)),
                      w_spec((d_hid, tn), 2),
                      pl.BlockSpec((tm, 1), lambda c, i, be, nu: (blk(i, nu), 0))],
            out_specs=pl.BlockSpec((tm, tn), lambda c, i, be, nu: (i, c)),
            scratch_shapes=[pltpu.VMEM((d_hid, tn), bf16)]),
        out_shape=jax.ShapeDtypeStruct((r_rows, d), f32),
        compiler_params=_cparams("arbitrary", "arbitrary"), name=name + "_down")(block_e, n_used, act, w2, row_w)


def _combine_body(pos_ref, y_ref, x1_ref, sh_ref, g2_ref, lg_ref, lb_ref, o_ref, buf, sem, *, tc, alpha):
    base = pl.program_id(0) * (TOP_K * tc)

    def row_copy(k, t, src_row):
        return pltpu.make_async_copy(y_ref.at[pl.ds(src_row, 1), :], buf.at[k, pl.ds(t, 1), :], sem)

    for k in range(TOP_K):
        def start(t, carry, k=k):
            row_copy(k, t, pos_ref[base + k * tc + t]).start()
            return carry
        lax.fori_loop(0, tc, start, 0)
    for k in range(TOP_K):
        def wait(t, carry, k=k):
            row_copy(k, t, 0).wait()
            return carry
        lax.fori_loop(0, tc, wait, 0)
    ffn = buf[0]
    for k in range(1, TOP_K):
        ffn = ffn + buf[k]
    ffn = ffn + sh_ref[...]
    o_ref[...] = _layer_norm(alpha * x1_ref[...] + g2_ref[...] * ffn, lg_ref[...], lb_ref[...])


def _combine(y_sorted, pos, x1, shared, grp, p, l, *, name):
    dd = _derived()
    d = D_MODEL
    rows = grp.rows
    tc = _tile(rows, T_COMBINE)
    n_t = rows // tc
    pos_flat = pos.reshape(TOP_K, n_t, tc).transpose(1, 0, 2).reshape(-1)
    tiles = grp.rows_per_seq // tc if not grp.per_row else None

    def mod_spec(which):
        if grp.per_row:
            return pl.BlockSpec((tc, d), lambda i, pr: (i, which))
        return pl.BlockSpec((None, 1, d), lambda i, pr: (i // tiles, 0, which))

    row = pl.BlockSpec((tc, d), lambda i, pr: (i, 0))
    par = pl.BlockSpec((None, 1, d), lambda i, pr: (l, 0, 0))
    return pl.pallas_call(
        functools.partial(_combine_body, tc=tc, alpha=dd["dn_alpha"]),
        grid_spec=pltpu.PrefetchScalarGridSpec(
            num_scalar_prefetch=1, grid=(n_t,),
            in_specs=[pl.BlockSpec(memory_space=pl.ANY), row, row, mod_spec(5), par, par],
            out_specs=row,
            scratch_shapes=[pltpu.VMEM((TOP_K, tc, d), f32), pltpu.SemaphoreType.DMA(())]),
        out_shape=jax.ShapeDtypeStruct((rows, d), f32),
        compiler_params=_cparams("arbitrary"), name=name,
    )(pos_flat, y_sorted, x1, shared, grp.mod, p["ln2_g"].reshape(DEPTH, 1, d), p["ln2_b"].reshape(DEPTH, 1, d))


def _routing_tables(topi, topw, rank, counts, tm):
    n_tok = topi.shape[1]
    m = n_tok * TOP_K
    n_blocks = -(-m // tm) + N_EXPERTS
    padded = (counts + tm - 1) // tm * tm
    ends = jnp.cumsum(padded)
    pos = (ends - padded)[topi] + rank
    tok = jnp.broadcast_to(jnp.arange(n_tok, dtype=jnp.int32)[None, :], pos.shape)
    slot_tok = jnp.zeros((n_blocks * tm,), jnp.int32).at[pos.reshape(-1)].set(tok.reshape(-1))
    slot_w = jnp.zeros((n_blocks * tm,), f32).at[pos.reshape(-1)].set(topw.reshape(-1))
    block_e = jnp.minimum(jnp.searchsorted(ends, jnp.arange(n_blocks, dtype=jnp.int32) * tm, side="right"),
                          N_EXPERTS - 1).astype(jnp.int32)
    n_used = (ends[-1] // tm).astype(jnp.int32).reshape(1)
    return pos, slot_tok, slot_w.reshape(-1, 1), block_e, n_used


def _token_mixer(x, grp, p, l, rel_bias, *, tag, prompt, n_seq, t_seq, conv_prev8, h0, caches, t_new):
    h = _modulate(x, grp, name=f"modulate_{tag}")
    z = _matmul(h, p["w_in"], l, name=f"in_proj_{tag}")
    hr, h_last = _rglru(z, conv_prev8, h0, p, l, n_seq=n_seq, t_seq=t_seq,
                        t_last=(t_seq - 1) % _tile(t_seq, T_SCAN) if prompt else t_new - 1,
                        name=f"rglru_{tag}")
    if prompt:
        att = _attn_prompt(z, rel_bias, n_seq=n_seq, s_len=t_seq, name=f"attn_{tag}")
    else:
        att = _attn_sample(z.reshape(n_seq, SAMPLE_ROWS, z.shape[1]), caches, rel_bias, l,
                           n_seq=n_seq, t_new=t_new, name=f"attn_{tag}")
    merged = _merge(hr, att, z, p, l, name=f"merge_{tag}")
    mix = _matmul(merged, p["w_o"], l, name=f"out_proj_{tag}")
    return mix, z, h_last


def _layer(l, xp, xs, gp, gs, p, rel_bias, caches, state_h, state_conv, dims):
    dd = _derived()
    n_p, s_len, n_s, t_new = dims
    d = D_MODEL
    d_rnn, gw, qkv = dd["d_rnn"], dd["gw"], dd["qkv"]

    zeros_prev = jnp.zeros((n_p, SUBLANE, d_rnn), f32)
    mix_p, z_p, hl_p = _token_mixer(xp, gp, p, l, rel_bias, tag="p", prompt=True, n_seq=n_p, t_seq=s_len,
                                    conv_prev8=zeros_prev, h0=jnp.zeros((n_p, d_rnn), f32),
                                    caches=None, t_new=None)
    prev_s = jnp.pad(state_conv[:, l], ((0, 0), (SUBLANE - (CONV_W - 1), 0), (0, 0)))
    mix_s, z_s, hl_s = _token_mixer(xs, gs, p, l, rel_bias, tag="s", prompt=False, n_seq=n_s,
                                    t_seq=SAMPLE_ROWS, conv_prev8=prev_s, h0=state_h[:, l],
                                    caches=caches, t_new=t_new)

    x1_p, hp_p, ti_p, tw_p, rk_p, cnt_p = _ln_router(xp, mix_p, gp, p, l, jnp.zeros((N_EXPERTS, LANE), f32),
                                                     name="ln_router_p")
    x1_s, hp_s, ti_s, tw_s, rk_s, cnt = _ln_router(xs, mix_s, gs, p, l, cnt_p, name="ln_router_s")
    topi = jnp.concatenate([ti_p, ti_s], axis=1)
    topw = jnp.concatenate([tw_p, tw_s], axis=1)
    rank = jnp.concatenate([rk_p, rk_s], axis=1)
    counts = cnt[:, 0].astype(jnp.int32)
    pos, slot_tok, slot_w, block_e, n_used = _routing_tables(topi, topw, rank, counts, TM_EXPERT)

    hp_all = jnp.concatenate([hp_p, hp_s], axis=0)
    x_sorted = _gather_rows(hp_all, slot_tok, name="moe_gather")
    y_sorted = _expert_ffn(x_sorted, p["w_e1"], p["w_e3"], p["w_e2"], (l,), block_e, n_used, slot_w,
                           tm=TM_EXPERT, name="moe_experts")

    def shared(hp, tag):
        rows = hp.shape[0]
        tm = _tile(rows, TM_EXPERT)
        nb = rows // tm
        return _expert_ffn(hp, p["w_s1"], p["w_s3"], p["w_s2"], (), jnp.full((nb,), l, jnp.int32),
                           jnp.full((1,), nb, jnp.int32), jnp.ones((rows, 1), f32), tm=tm,
                           name=f"shared_{tag}")

    n_tok_p = gp.rows
    x2_p = _combine(y_sorted, pos[:, :n_tok_p], x1_p, shared(hp_p, "p"), gp, p, l, name="combine_p")
    x2_s = _combine(y_sorted, pos[:, n_tok_p:], x1_s, shared(hp_s, "s"), gs, p, l, name="combine_s")

    z_p3 = z_p.reshape(n_p, s_len, -1)
    z_s3 = z_s.reshape(n_s, SAMPLE_ROWS, -1)
    kv_p, kv_s = [], []
    for g, (win, _) in enumerate(ATT_GROUPS):
        keep = min(win, s_len)
        k0 = d_rnn + qkv + g * gw
        v0 = d_rnn + 2 * qkv + g * gw

        def heads(zz, c0, rows):
            return zz[:, rows, c0:c0 + gw].reshape(zz.shape[0], -1, HEADS_PER_GROUP, HEAD_DIM)

        rp = slice(s_len - keep, s_len)
        kv_p.append(jnp.stack([heads(z_p3, k0, rp), heads(z_p3, v0, rp)], axis=1))
        rs = slice(0, t_new)
        kv_s.append(jnp.stack([heads(z_s3, k0, rs), heads(z_s3, v0, rs)], axis=1))
    conv_p = z_p3[:, s_len - (CONV_W - 1):, :d_rnn]
    conv_s = z_s3[:, t_new - (CONV_W - 1):t_new, :d_rnn] if t_new >= CONV_W - 1 else None
    return x2_p, x2_s, kv_p, kv_s, hl_p, hl_s, conv_p, conv_s


def kernel(x_prompt, x_sample, c_prompt, c_sample, cache_kv_w128, cache_kv_w512, cache_kv_w2048,
           state_rglru_h, state_conv, rel_bias, w_mod, b_mod, w_in, conv_w, conv_b, lru_wa, lru_ba,
           lru_wx, lru_bx, lru_lam, w_pa, w_pb, w_o, ln1_g, ln1_b, w_router, b_router, w_e1, w_e3,
           w_e2, w_s1, w_s3, w_s2, ln2_g, ln2_b):
    p = dict(w_in=w_in, conv_w=conv_w, conv_b=conv_b, lru_wa=lru_wa, lru_ba=lru_ba, lru_wx=lru_wx,
             lru_bx=lru_bx, lru_lam=lru_lam, w_pa=w_pa, w_pb=w_pb, w_o=w_o, ln1_g=ln1_g, ln1_b=ln1_b,
             w_router=w_router, b_router=b_router, w_e1=w_e1, w_e3=w_e3, w_e2=w_e2, w_s1=w_s1,
             w_s3=w_s3, w_s2=w_s2, ln2_g=ln2_g, ln2_b=ln2_b)
    caches = (cache_kv_w128, cache_kv_w512, cache_kv_w2048)
    n_p, s_len, d = x_prompt.shape
    n_s, t_new, _ = x_sample.shape
    assert d == D_MODEL and t_new <= SAMPLE_ROWS and t_new >= CONV_W - 1
    dims = (n_p, s_len, n_s, t_new)

    n_c = n_p + n_s
    c_all = jnp.pad(jnp.concatenate([c_prompt, c_sample], axis=0), ((0, -n_c % SUBLANE), (0, 0)))

    xp = x_prompt.reshape(n_p * s_len, d)
    xs = jnp.pad(x_sample, ((0, 0), (0, SAMPLE_ROWS - t_new), (0, 0))).reshape(n_s * SAMPLE_ROWS, d)
    outs = dict(kv_p=[], kv_s=[], h_p=[], h_s=[], conv_p=[], conv_s=[])
    for l in range(DEPTH):
        mod = _matmul(c_all, w_mod, l, name="adaln_mod", bias=b_mod, silu_a=True)
        gp = _Group(n_p * s_len, _tile(s_len, TM_TOKEN), mod[:n_p].reshape(n_p, 1, 6 * d), False, s_len)
        gs = _Group(n_s * SAMPLE_ROWS, n_s * SAMPLE_ROWS, jnp.repeat(mod[n_p:n_c], SAMPLE_ROWS, axis=0),
                    True, SAMPLE_ROWS)
        xp, xs, kv_p, kv_s, hl_p, hl_s, conv_p, conv_s = _layer(
            l, xp, xs, gp, gs, p, rel_bias, caches, state_rglru_h, state_conv, dims)
        for key, val in zip(("kv_p", "kv_s", "h_p", "h_s", "conv_p", "conv_s"),
                            (kv_p, kv_s, hl_p, hl_s, conv_p, conv_s)):
            outs[key].append(val)

    def stack_kv(per_layer, g):
        return jnp.stack([r[g] for r in per_layer], axis=1)

    y_prompt = xp.reshape(n_p, s_len, d)
    y_sample = xs.reshape(n_s, SAMPLE_ROWS, d)[:, :t_new]
    return (y_prompt, y_sample,
            stack_kv(outs["kv_p"], 0), stack_kv(outs["kv_p"], 1), stack_kv(outs["kv_p"], 2),
            jnp.stack(outs["h_p"], axis=1), jnp.stack(outs["conv_p"], axis=1),
            stack_kv(outs["kv_s"], 0), stack_kv(outs["kv_s"], 1), stack_kv(outs["kv_s"], 2),
            jnp.stack(outs["h_s"], axis=1), jnp.stack(outs["conv_s"], axis=1))
```

```python
import functools
import math

import numpy as np
import jax
import jax.numpy as jnp
from jax import lax
from jax.experimental import pallas as pl
from jax.experimental.pallas import tpu as pltpu

bf16 = jnp.bfloat16
f32 = jnp.float32

D_MODEL = 4096
DEPTH = 2
LRU_BLOCKS = 16
CONV_W = 4
LRU_C = 8.0
HEAD_DIM = 128
HEADS_PER_GROUP = 8
ATT_GROUPS = ((128, 1), (512, 4), (2048, 16))
REL_BUCKETS = 32
REL_MAX_DIST = 2048
N_EXPERTS = 64
N_EXPERT_GROUPS = 8
TOPK_GROUPS = 4
TOP_K = 8
ROUTED_SCALE = 2.5
LN_EPS = 1e-5

V7X_VMEM_LIMIT_BYTES = 56 * 1024 * 1024
LANE = 128
SUBLANE = 8
SAMPLE_ROWS = 8
TM_MATMUL = 1024
TN_MATMUL = 512
TM_MERGE = 1024
TN_MERGE = 256
TM_TOKEN = 256
T_SCAN = 256
TM_EXPERT = 512
CH_EXPERT = 256
TN_EXPERT = 2048
T_GATHER = 512
T_COMBINE = 128


def _derived():
    d_rnn = D_MODEL
    gw = HEADS_PER_GROUP * HEAD_DIM
    qkv = len(ATT_GROUPS) * gw
    return dict(d_rnn=d_rnn, gw=gw, qkv=qkv, in_cols=d_rnn + 3 * qkv + 2 * D_MODEL,
                dn_alpha=(2 * DEPTH) ** 0.25, att_scale=HEAD_DIM ** -0.5,
                lru_block=d_rnn // LRU_BLOCKS)


def _cparams(*sem):
    return pltpu.CompilerParams(dimension_semantics=sem, vmem_limit_bytes=V7X_VMEM_LIMIT_BYTES)


def _tile(n, pref):
    return pref if n % pref == 0 else n


def _nt_dot(a, b):
    return lax.dot_general(a, b, (((1,), (1,)), ((), ())), preferred_element_type=f32)


def _mm_body(*refs, silu_a, has_bias):
    if has_bias:
        a_ref, b_ref, bias_ref, o_ref = refs
    else:
        a_ref, b_ref, o_ref = refs
    a = a_ref[...]
    if silu_a:
        a = a * jax.nn.sigmoid(a)
    acc = jnp.dot(a.astype(bf16), b_ref[...].astype(bf16), preferred_element_type=f32)
    if has_bias:
        acc = acc + bias_ref[...]
    o_ref[...] = acc.astype(o_ref.dtype)


def _matmul(a, w, l, *, name, bias=None, silu_a=False, out_dtype=f32):
    m, k = a.shape
    n = w.shape[-1]
    tm, tn = _tile(m, TM_MATMUL), _tile(n, TN_MATMUL)
    in_specs = [pl.BlockSpec((tm, k), lambda i, j: (i, 0)),
                pl.BlockSpec((None, k, tn), lambda i, j: (l, 0, j))]
    args = [a, w]
    if bias is not None:
        in_specs.append(pl.BlockSpec((None, 1, tn), lambda i, j: (l, 0, j)))
        args.append(bias.reshape(bias.shape[0], 1, n))
    return pl.pallas_call(
        functools.partial(_mm_body, silu_a=silu_a, has_bias=bias is not None),
        grid=(m // tm, n // tn), in_specs=in_specs,
        out_specs=pl.BlockSpec((tm, tn), lambda i, j: (i, j)),
        out_shape=jax.ShapeDtypeStruct((m, n), out_dtype),
        compiler_params=_cparams("parallel", "arbitrary"), name=name)(*args)


class _Group:
    def __init__(self, rows, tm, mod, per_row, rows_per_seq):
        self.rows, self.tm, self.mod, self.per_row, self.rows_per_seq = rows, tm, mod, per_row, rows_per_seq

    def mod_spec(self, which):
        d = D_MODEL
        if self.per_row:
            return pl.BlockSpec((self.tm, d), lambda i: (i, which))
        tiles = self.rows_per_seq // self.tm
        return pl.BlockSpec((None, 1, d), lambda i: (i // tiles, 0, which))

    def row_spec(self, width):
        return pl.BlockSpec((self.tm, width), lambda i: (i, 0))


def _param_spec(l, width):
    return pl.BlockSpec((None, 1, width), lambda i: (l, 0, 0))


def _modulate_body(x_ref, sc_ref, sh_ref, o_ref):
    o_ref[...] = (x_ref[...] * (1.0 + sc_ref[...]) + sh_ref[...]).astype(o_ref.dtype)


def _modulate(x, grp, *, name):
    d = D_MODEL
    return pl.pallas_call(
        _modulate_body, grid=(grp.rows // grp.tm,),
        in_specs=[grp.row_spec(d), grp.mod_spec(1), grp.mod_spec(0)],
        out_specs=grp.row_spec(d), out_shape=jax.ShapeDtypeStruct((grp.rows, d), bf16),
        compiler_params=_cparams("parallel"), name=name)(x, grp.mod, grp.mod)


def _log_sigmoid(x):
    return -(jnp.maximum(-x, 0.0) + jnp.log1p(jnp.exp(-jnp.abs(x))))


def _neg_expm1(y):
    t = jnp.tanh(0.5 * y)
    return -2.0 * t / (1.0 - t)


def _rglru_body(x_ref, cp_ref, h0_ref, cw_ref, cb_ref, wa_ref, ba_ref, wx_ref, bx_ref, lam_ref,
                hr_ref, hl_ref, prev_scr, h_scr, *, tt, t_last):
    i = pl.program_id(2)

    @pl.when(i == 0)
    def _():
        if tt > SUBLANE:
            prev_scr[0:tt - SUBLANE, :] = jnp.zeros((tt - SUBLANE, prev_scr.shape[1]), f32)
        prev_scr[tt - SUBLANE:tt, :] = cp_ref[...]
        h_scr[...] = h0_ref[...]

    x = x_ref[...]
    prev = prev_scr[...]
    row = lax.broadcasted_iota(jnp.int32, x.shape, 0)

    def shifted(k):
        return jnp.where(row < k, pltpu.roll(prev, k, 0), pltpu.roll(x, k, 0))

    cw = cw_ref[...]
    xc = cb_ref[...] + shifted(3) * cw[0:1]
    xc = xc + shifted(2) * cw[1:2]
    xc = xc + shifted(1) * cw[2:3]
    xc = xc + x * cw[3:4]
    prev_scr[...] = x

    xcb = xc.astype(bf16)
    r = jax.nn.sigmoid(jnp.dot(xcb, wa_ref[...].astype(bf16), preferred_element_type=f32) + ba_ref[...])
    gi = jax.nn.sigmoid(jnp.dot(xcb, wx_ref[...].astype(bf16), preferred_element_type=f32) + bx_ref[...])
    log_a = LRU_C * r * _log_sigmoid(lam_ref[...])
    a = jnp.exp(log_a)
    b = jnp.sqrt(_neg_expm1(2.0 * log_a)) * (gi * xc)

    s = 1
    while s < tt:
        a_sh = jnp.where(row < s, 1.0, pltpu.roll(a, s, 0))
        b_sh = jnp.where(row < s, 0.0, pltpu.roll(b, s, 0))
        b = a * b_sh + b
        a = a * a_sh
        s *= 2
    h = a * h_scr[...] + b
    hr_ref[...] = h.astype(hr_ref.dtype)
    h_scr[...] = h[tt - 1:tt, :]
    hl_ref[...] = h[t_last:t_last + 1, :]


def _rglru(z, conv_prev8, h0, p, l, *, n_seq, t_seq, t_last, name):
    dd = _derived()
    lb = dd["lru_block"]
    tt = _tile(t_seq, T_SCAN)
    n_t = t_seq // tt
    d_rnn = dd["d_rnn"]

    def vec(arr):
        return arr.reshape(DEPTH, 1, d_rnn)

    vspec = pl.BlockSpec((None, 1, lb), lambda b, n, i: (l, 0, n))
    wspec = pl.BlockSpec((None, None, lb, lb), lambda b, n, i: (l, n, 0, 0))
    hr, h_last = pl.pallas_call(
        functools.partial(_rglru_body, tt=tt, t_last=t_last),
        grid=(n_seq, LRU_BLOCKS, n_t),
        in_specs=[pl.BlockSpec((tt, lb), lambda b, n, i: (b * n_t + i, n)),
                  pl.BlockSpec((None, SUBLANE, lb), lambda b, n, i: (b, 0, n)),
                  pl.BlockSpec((None, 1, lb), lambda b, n, i: (b, 0, n)),
                  pl.BlockSpec((None, CONV_W, lb), lambda b, n, i: (l, 0, n)),
                  vspec, wspec, vspec, wspec, vspec, vspec],
        out_specs=[pl.BlockSpec((tt, lb), lambda b, n, i: (b * n_t + i, n)),
                   pl.BlockSpec((None, 1, lb), lambda b, n, i: (b, 0, n))],
        out_shape=[jax.ShapeDtypeStruct((n_seq * t_seq, d_rnn), bf16),
                   jax.ShapeDtypeStruct((n_seq, 1, d_rnn), f32)],
        scratch_shapes=[pltpu.VMEM((tt, lb), f32), pltpu.VMEM((1, lb), f32)],
        compiler_params=_cparams("parallel", "parallel", "arbitrary"), name=name,
    )(z, conv_prev8, h0.reshape(n_seq, 1, d_rnn), p["conv_w"], vec(p["conv_b"]), p["lru_wa"],
      vec(p["lru_ba"]), p["lru_wx"], vec(p["lru_bx"]), vec(p["lru_lam"]))
    return hr, h_last.reshape(n_seq, d_rnn)


def _rel_bucket(dist):
    dist = np.asarray(dist, np.int32)
    max_exact = REL_BUCKETS // 2
    df = np.maximum(dist, 1).astype(np.float32)
    large = max_exact + (np.log(df / np.float32(max_exact)) / np.float32(math.log(REL_MAX_DIST / max_exact))
                         * np.float32(REL_BUCKETS - max_exact)).astype(np.int32)
    large = np.minimum(large, REL_BUCKETS - 1)
    return np.where(dist < max_exact, dist, large)


def _bias_lookup(tab, bucket, ok):
    flat = jnp.asarray(bucket.reshape(-1), jnp.int32)
    onehot = (flat[:, None] == jnp.arange(REL_BUCKETS, dtype=jnp.int32)[None, :]).astype(f32)
    vals = jnp.dot(onehot, tab.astype(f32), precision=lax.Precision.HIGHEST)
    vals = vals.reshape(bucket.shape + (tab.shape[1],))
    return jnp.where(jnp.asarray(ok)[..., None], vals, -jnp.inf)


def _prompt_bias(rel_bias, g, band, dil):
    rel = np.arange(band)[:, None] + band - np.arange(2 * band)[None, :]
    in_band = (rel >= 0) & (rel <= band)
    tab = rel_bias[:, g * HEADS_PER_GROUP:(g + 1) * HEADS_PER_GROUP]
    vals = _bias_lookup(tab, _rel_bucket(np.clip(rel, 0, band) * dil), in_band)
    return vals.transpose(2, 0, 1)


def _sample_bias(rel_bias, g, win, dil, t_new, lb_cache):
    nk = win // dil + 1
    t = np.arange(SAMPLE_ROWS)[:, None]
    tab = rel_bias[:, g * HEADS_PER_GROUP:(g + 1) * HEADS_PER_GROUP]

    def table(delta):
        ok = (delta >= 0) & (delta % dil == 0) & (delta // dil < nk)
        vals = _bias_lookup(tab, _rel_bucket(np.maximum(delta, 0)), ok)
        vals = jnp.where(jnp.asarray(t < t_new)[:, :, None], vals, 0.0)
        return vals.transpose(2, 0, 1)

    cache = table(lb_cache + t - np.arange(lb_cache)[None, :])
    j = np.arange(SAMPLE_ROWS)[None, :]
    new = table(np.where(j < t_new, t - j, -1))
    return cache, new


def _attn_prompt_body(*refs, s_len, scale):
    n_g = len(ATT_GROUPS)
    o_ref, o_scr, l_scr = refs[4 * n_g:]
    for g, (win, dil) in enumerate(ATT_GROUPS):
        q_ref, k_ref, v_ref, b_ref = refs[4 * g:4 * g + 4]
        band = win // dil
        nb = (s_len // dil) // band
        bias = b_ref[...]
        for r in range(dil):
            for blk in range(nb):
                def rows(first_blk, n_rows):
                    start = r + first_blk * band * dil
                    return pl.ds(start, n_rows) if dil == 1 else pl.ds(start, n_rows, stride=dil)
                qrows = rows(blk, band)
                krows, bb = (qrows, bias[:, band:]) if blk == 0 else (rows(blk - 1, 2 * band), bias)
                q = q_ref[qrows, :].astype(bf16)
                k = k_ref[krows, :].astype(bf16)
                v = v_ref[krows, :].astype(bf16)
                s = _nt_dot(q, k) * scale + bb
                mx = jnp.max(s, axis=-1, keepdims=True)
                p = jnp.exp(s - mx)
                den = jnp.sum(p, axis=-1, keepdims=True)
                o = jnp.dot((p / den).astype(bf16), v, preferred_element_type=f32)
                o_scr[g, qrows, :] = o
                l_scr[g, qrows, :] = jnp.broadcast_to(mx + jnp.log(den), o.shape)
    ch = 256
    for c in range(s_len // ch):
        sl = pl.ds(c * ch, ch)
        ls = [l_scr[g, sl, :] for g in range(n_g)]
        m = functools.reduce(jnp.maximum, ls)
        es = [jnp.exp(x - m) for x in ls]
        tot = functools.reduce(lambda a, b: a + b, es)
        out = functools.reduce(lambda a, b: a + b, [(es[g] / tot) * o_scr[g, sl, :] for g in range(n_g)])
        o_ref[sl, :] = out.astype(o_ref.dtype)


def _attn_prompt(z, biases, *, n_seq, s_len, name):
    dd = _derived()
    gw, d_rnn, qkv = dd["gw"], dd["d_rnn"], dd["qkv"]
    in_specs, args = [], []
    for g, (win, dil) in enumerate(ATT_GROUPS):
        band = win // dil
        assert (s_len // dil) % band == 0
        for part in range(3):
            c0 = (d_rnn + part * qkv + g * gw) // HEAD_DIM
            in_specs.append(pl.BlockSpec((s_len, HEAD_DIM), lambda b, h, c0=c0: (b, c0 + h)))
            args.append(z)
        in_specs.append(pl.BlockSpec((None, band, 2 * band), lambda b, h: (h, 0, 0)))
        args.append(biases[g])
    n_g = len(ATT_GROUPS)
    return pl.pallas_call(
        functools.partial(_attn_prompt_body, s_len=s_len, scale=dd["att_scale"]),
        grid=(n_seq, HEADS_PER_GROUP), in_specs=in_specs,
        out_specs=pl.BlockSpec((s_len, HEAD_DIM), lambda b, h: (b, h)),
        out_shape=jax.ShapeDtypeStruct((n_seq * s_len, gw), bf16),
        scratch_shapes=[pltpu.VMEM((n_g, s_len, HEAD_DIM), f32), pltpu.VMEM((n_g, s_len, HEAD_DIM), f32)],
        compiler_params=_cparams("parallel", "parallel"), name=name)(*args)


def _attn_sample_body(*refs, scale):
    n_g = len(ATT_GROUPS)
    o_ref = refs[7 * n_g]
    outs, lses = [], []
    for g in range(n_g):
        q_ref, kn_ref, vn_ref, kc_ref, vc_ref, bc_ref, bn_ref = refs[7 * g:7 * g + 7]
        q = q_ref[...].astype(bf16)
        s_c = _nt_dot(q, kc_ref[...].astype(bf16)) * scale + bc_ref[...]
        s_n = _nt_dot(q, kn_ref[...].astype(bf16)) * scale + bn_ref[...]
        mx = jnp.maximum(jnp.max(s_c, axis=-1, keepdims=True), jnp.max(s_n, axis=-1, keepdims=True))
        p_c = jnp.exp(s_c - mx)
        p_n = jnp.exp(s_n - mx)
        den = jnp.sum(p_c, axis=-1, keepdims=True) + jnp.sum(p_n, axis=-1, keepdims=True)
        o = jnp.dot((p_c / den).astype(bf16), vc_ref[...].astype(bf16), preferred_element_type=f32)
        o = o + jnp.dot((p_n / den).astype(bf16), vn_ref[...].astype(bf16), preferred_element_type=f32)
        outs.append(o)
        lses.append(mx + jnp.log(den))
    m = functools.reduce(jnp.maximum, lses)
    es = [jnp.exp(x - m) for x in lses]
    tot = functools.reduce(lambda a, b: a + b, es)
    o_ref[...] = functools.reduce(lambda a, b: a + b, [(es[g] / tot) * outs[g] for g in range(n_g)])


def _attn_sample(z3, caches, biases, l, *, n_seq, t_new, name):
    dd = _derived()
    gw, d_rnn, qkv = dd["gw"], dd["d_rnn"], dd["qkv"]
    in_specs, args = [], []
    for g, (win, dil) in enumerate(ATT_GROUPS):
        cache = caches[g]
        lb_cache = cache.shape[3]
        cache = cache.reshape(n_seq, DEPTH, 2, lb_cache, gw)
        for part in range(3):
            c0 = (d_rnn + part * qkv + g * gw) // HEAD_DIM
            in_specs.append(pl.BlockSpec((None, SAMPLE_ROWS, HEAD_DIM), lambda b, h, c0=c0: (b, 0, c0 + h)))
            args.append(z3)
        for kv in range(2):
            in_specs.append(pl.BlockSpec((None, None, None, lb_cache, HEAD_DIM),
                                         lambda b, h, kv=kv: (b, l, kv, 0, h)))
            args.append(cache)
        b_cache, b_new = biases[g]
        in_specs.append(pl.BlockSpec((None, SAMPLE_ROWS, lb_cache), lambda b, h: (h, 0, 0)))
        in_specs.append(pl.BlockSpec((None, SAMPLE_ROWS, SAMPLE_ROWS), lambda b, h: (h, 0, 0)))
        args += [b_cache, b_new]
    out = pl.pallas_call(
        functools.partial(_attn_sample_body, scale=dd["att_scale"]),
        grid=(n_seq, HEADS_PER_GROUP), in_specs=in_specs,
        out_specs=pl.BlockSpec((None, SAMPLE_ROWS, HEAD_DIM), lambda b, h: (b, 0, h)),
        out_shape=jax.ShapeDtypeStruct((n_seq, SAMPLE_ROWS, gw), f32),
        compiler_params=_cparams("parallel", "parallel"), name=name)(*args)
    return out.reshape(n_seq * SAMPLE_ROWS, gw)


def _merge_body(hr_ref, att_ref, wa_ref, wb_ref, ga_ref, gb_ref, o_ref):
    pa = jnp.dot(hr_ref[...].astype(bf16), wa_ref[...].astype(bf16), preferred_element_type=f32)
    pb = jnp.dot(att_ref[...].astype(bf16), wb_ref[...].astype(bf16), preferred_element_type=f32)
    o_ref[...] = (jax.nn.sigmoid(ga_ref[...]) * pa + jax.nn.sigmoid(gb_ref[...]) * pb).astype(o_ref.dtype)


def _merge(hr, att, z, p, l, *, name):
    dd = _derived()
    m = hr.shape[0]
    d = D_MODEL
    tm, tn = _tile(m, TM_MERGE), _tile(d, TN_MERGE)
    ga0 = (dd["d_rnn"] + 3 * dd["qkv"]) // tn
    gb0 = ga0 + d // tn
    return pl.pallas_call(
        _merge_body, grid=(m // tm, d // tn),
        in_specs=[pl.BlockSpec((tm, hr.shape[1]), lambda i, j: (i, 0)),
                  pl.BlockSpec((tm, att.shape[1]), lambda i, j: (i, 0)),
                  pl.BlockSpec((None, hr.shape[1], tn), lambda i, j: (l, 0, j)),
                  pl.BlockSpec((None, att.shape[1], tn), lambda i, j: (l, 0, j)),
                  pl.BlockSpec((tm, tn), lambda i, j: (i, ga0 + j)),
                  pl.BlockSpec((tm, tn), lambda i, j: (i, gb0 + j))],
        out_specs=pl.BlockSpec((tm, tn), lambda i, j: (i, j)),
        out_shape=jax.ShapeDtypeStruct((m, d), bf16),
        compiler_params=_cparams("parallel", "arbitrary"), name=name)(hr, att, p["w_pa"], p["w_pb"], z, z)


def _layer_norm(v, g, b):
    mu = jnp.mean(v, axis=-1, keepdims=True)
    var = jnp.mean(jnp.square(v - mu), axis=-1, keepdims=True)
    return (v - mu) * lax.rsqrt(var + LN_EPS) * g + b


def _pack_halves(h):
    half = h.shape[1] // 2
    lo = lax.bitcast_convert_type(h[:, :half].astype(bf16).astype(f32), jnp.uint32)
    hi = lax.bitcast_convert_type(h[:, half:].astype(bf16).astype(f32), jnp.uint32)
    return (lo >> 16) | (hi & jnp.uint32(0xFFFF0000))


def _unpack_halves(xu):
    lo = lax.bitcast_convert_type(xu << 16, f32).astype(bf16)
    hi = lax.bitcast_convert_type(xu & jnp.uint32(0xFFFF0000), f32).astype(bf16)
    return lo, hi


def _split_bf16(x):
    hi = x.astype(bf16)
    return hi, (x - hi.astype(f32)).astype(bf16)


def _first_max(vals, idx):
    m = functools.reduce(jnp.maximum, [jnp.max(v, axis=0, keepdims=True) for v in vals])
    big = float(N_EXPERTS)
    cand = [jnp.min(jnp.where(v == m, i, big), axis=0, keepdims=True) for v, i in zip(vals, idx)]
    return m, functools.reduce(jnp.minimum, cand)


def _ln_router_body(x_ref, mix_ref, g1_ref, sc_ref, sh_ref, lg_ref, lb_ref, wr_ref, br_ref, base_ref,
                    x1_ref, hp_ref, topi_ref, topw_ref, rank_ref, cnt_ref, *, alpha):
    i = pl.program_id(0)
    x1 = _layer_norm(alpha * x_ref[...] + g1_ref[...] * mix_ref[...], lg_ref[...], lb_ref[...])
    x1_ref[...] = x1
    h = x1 * (1.0 + sc_ref[...]) + sh_ref[...]
    hp_ref[...] = _pack_halves(h)

    h_hi, h_lo = _split_bf16(h)
    w_hi, w_lo = _split_bf16(wr_ref[...])
    logits = _nt_dot(w_hi, h_hi) + _nt_dot(w_hi, h_lo) + _nt_dot(w_lo, h_hi)
    s = jax.nn.sigmoid(logits)
    sel = s + br_ref[...]
    tm = s.shape[1]
    per = N_EXPERTS // N_EXPERT_GROUPS
    sub = lax.broadcasted_iota(jnp.int32, (per, tm), 0).astype(f32)
    sel_g = [sel[g * per:(g + 1) * per, :] for g in range(N_EXPERT_GROUPS)]
    s_g = [s[g * per:(g + 1) * per, :] for g in range(N_EXPERT_GROUPS)]
    idx_g = [sub + float(g * per) for g in range(N_EXPERT_GROUPS)]
    neg = -jnp.inf

    grp = []
    for g in range(N_EXPERT_GROUPS):
        m1, i1 = _first_max([sel_g[g]], [idx_g[g]])
        m2 = jnp.max(jnp.where(idx_g[g] == i1, neg, sel_g[g]), axis=0, keepdims=True)
        grp.append(m1 + m2)
    keep = [jnp.zeros((1, tm), jnp.bool_) for _ in range(N_EXPERT_GROUPS)]
    for _ in range(TOPK_GROUPS):
        m = functools.reduce(jnp.maximum, grp)
        found = jnp.zeros((1, tm), jnp.bool_)
        for g in range(N_EXPERT_GROUPS):
            hit = (grp[g] == m) & jnp.logical_not(found)
            found = found | hit
            keep[g] = keep[g] | hit
            grp[g] = jnp.where(hit, neg, grp[g])
    cand = [jnp.where(keep[g], sel_g[g], neg) for g in range(N_EXPERT_GROUPS)]

    @pl.when(i == 0)
    def _():
        cnt_ref[...] = base_ref[...]

    picked = [jnp.zeros((per, tm), f32) for _ in range(N_EXPERT_GROUPS)]
    top_i, top_w = [], []
    for _ in range(TOP_K):
        _, ik = _first_max(cand, idx_g)
        hits = [idx_g[g] == ik for g in range(N_EXPERT_GROUPS)]
        wk = functools.reduce(lambda a, b: a + b,
                              [jnp.sum(jnp.where(hits[g], s_g[g], 0.0), axis=0, keepdims=True)
                               for g in range(N_EXPERT_GROUPS)])
        for g in range(N_EXPERT_GROUPS):
            cand[g] = jnp.where(hits[g], neg, cand[g])
            picked[g] = jnp.where(hits[g], 1.0, picked[g])
        top_i.append(ik)
        top_w.append(wk)
    wsum = functools.reduce(lambda a, b: a + b, top_w)

    onehot = jnp.concatenate(picked, axis=0)
    r_i = lax.broadcasted_iota(jnp.int32, (tm, tm), 0)
    c_i = lax.broadcasted_iota(jnp.int32, (tm, tm), 1)
    before = (r_i < c_i).astype(bf16)
    prefix = jnp.dot(onehot.astype(bf16), before, preferred_element_type=f32) + cnt_ref[:, 0:1]
    cnt_ref[...] = cnt_ref[...] + jnp.sum(onehot, axis=1, keepdims=True)
    pre_g = [prefix[g * per:(g + 1) * per, :] for g in range(N_EXPERT_GROUPS)]
    for k in range(TOP_K):
        rk = functools.reduce(lambda a, b: a + b,
                              [jnp.sum(jnp.where(idx_g[g] == top_i[k], pre_g[g], 0.0), axis=0, keepdims=True)
                               for g in range(N_EXPERT_GROUPS)])
        topi_ref[k:k + 1, :] = top_i[k].astype(jnp.int32)
        topw_ref[k:k + 1, :] = top_w[k] / wsum * ROUTED_SCALE
        rank_ref[k:k + 1, :] = rk.astype(jnp.int32)


def _ln_router(x, mix, grp, p, l, base_counts, *, name):
    dd = _derived()
    d = D_MODEL
    rows, tm = grp.rows, grp.tm
    w_rt = jnp.swapaxes(p["w_router"], 1, 2)
    b_r = p["b_router"].reshape(DEPTH, N_EXPERTS, 1)
    tok = lambda dt: jax.ShapeDtypeStruct((TOP_K, rows), dt)
    tok_spec = pl.BlockSpec((TOP_K, tm), lambda i: (0, i))
    return pl.pallas_call(
        functools.partial(_ln_router_body, alpha=dd["dn_alpha"]),
        grid=(rows // tm,),
        in_specs=[grp.row_spec(d), grp.row_spec(d), grp.mod_spec(2), grp.mod_spec(4), grp.mod_spec(3),
                  _param_spec(l, d), _param_spec(l, d),
                  pl.BlockSpec((None, N_EXPERTS, d), lambda i: (l, 0, 0)),
                  pl.BlockSpec((None, N_EXPERTS, 1), lambda i: (l, 0, 0)),
                  pl.BlockSpec((N_EXPERTS, LANE), lambda i: (0, 0))],
        out_specs=[grp.row_spec(d), grp.row_spec(d // 2), tok_spec, tok_spec, tok_spec,
                   pl.BlockSpec((N_EXPERTS, LANE), lambda i: (0, 0))],
        out_shape=[jax.ShapeDtypeStruct((rows, d), f32), jax.ShapeDtypeStruct((rows, d // 2), jnp.uint32),
                   tok(jnp.int32), tok(f32), tok(jnp.int32),
                   jax.ShapeDtypeStruct((N_EXPERTS, LANE), f32)],
        compiler_params=_cparams("arbitrary"), name=name,
    )(x, mix, grp.mod, grp.mod, grp.mod, p["ln1_g"].reshape(DEPTH, 1, d), p["ln1_b"].reshape(DEPTH, 1, d),
      w_rt, b_r, base_counts)


def _gather_body(idx_ref, src_ref, o_ref, sem, *, tg):
    base = pl.program_id(0) * tg

    def row_copy(r, src_row):
        return pltpu.make_async_copy(src_ref.at[pl.ds(src_row, 1), :], o_ref.at[pl.ds(r, 1), :], sem)

    def start(j, carry):
        for q in range(2):
            r = 2 * j + q
            row_copy(r, idx_ref[base + r]).start(priority=q)
        return carry

    def wait(r, carry):
        row_copy(r, 0).wait()
        return carry

    lax.fori_loop(0, tg // 2, start, 0)
    lax.fori_loop(0, tg, wait, 0)


def _gather_rows(src, idx, *, name):
    r_out, width = idx.shape[0], src.shape[1]
    tg = _tile(r_out, T_GATHER)
    assert tg % 2 == 0
    return pl.pallas_call(
        functools.partial(_gather_body, tg=tg),
        grid_spec=pltpu.PrefetchScalarGridSpec(
            num_scalar_prefetch=1, grid=(r_out // tg,),
            in_specs=[pl.BlockSpec(memory_space=pl.ANY)],
            out_specs=pl.BlockSpec((tg, width), lambda i, idx_ref: (i, 0)),
            scratch_shapes=[pltpu.SemaphoreType.DMA(())]),
        out_shape=jax.ShapeDtypeStruct((r_out, width), src.dtype),
        compiler_params=_cparams("arbitrary"), name=name)(idx, src)


def _expert_changed(be_ref, i):
    return (i == 0) | (be_ref[i] != be_ref[jnp.maximum(i - 1, 0)])


def _ffn_a_body(be_ref, nu_ref, x_ref, w1_ref, w3_ref, o_ref, w1_scr, w3_scr):
    i = pl.program_id(1)

    @pl.when(i < nu_ref[0])
    def _():
        @pl.when(_expert_changed(be_ref, i))
        def _():
            w1_scr[...] = w1_ref[...].astype(bf16)
            w3_scr[...] = w3_ref[...].astype(bf16)

        lo, hi = _unpack_halves(x_ref[...])
        half = lo.shape[1]
        h1 = (jnp.dot(lo, w1_scr[0:half, :], preferred_element_type=f32)
              + jnp.dot(hi, w1_scr[half:, :], preferred_element_type=f32))
        h3 = (jnp.dot(lo, w3_scr[0:half, :], preferred_element_type=f32)
              + jnp.dot(hi, w3_scr[half:, :], preferred_element_type=f32))
        o_ref[...] = (h1 * jax.nn.sigmoid(h1) * h3).astype(o_ref.dtype)

    @pl.when(i >= nu_ref[0])
    def _():
        o_ref[...] = jnp.zeros(o_ref.shape, o_ref.dtype)


def _ffn_b_body(be_ref, nu_ref, a_ref, w2_ref, o_ref, w2_scr):
    i = pl.program_id(1)

    @pl.when(i < nu_ref[0])
    def _():
        @pl.when(_expert_changed(be_ref, i))
        def _():
            w2_scr[...] = w2_ref[...].astype(bf16)

        o_ref[...] = jnp.dot(a_ref[...], w2_scr[...], preferred_element_type=f32)

    @pl.when(i >= nu_ref[0])
    def _():
        o_ref[...] = jnp.zeros(o_ref.shape, o_ref.dtype)


def _expert_ffn(xp, w1, w3, w2, lead, block_e, n_used, *, tm, name):
    r_rows, half = xp.shape
    d = 2 * half
    d_hid = w1.shape[-1]
    n_blocks = r_rows // tm
    ch, tn = _tile(d_hid, CH_EXPERT), _tile(d, TN_EXPERT)
    nl = len(lead)

    def blk(i, nu_ref):
        return jnp.minimum(i, nu_ref[0] - 1)

    def w_spec(shape, col_axis):
        def index(c, i, be_ref, nu_ref):
            e = be_ref[blk(i, nu_ref)]
            return lead + ((e, 0, c) if col_axis == 2 else (e, c, 0))
        return pl.BlockSpec((None,) * (nl + 1) + shape, index)

    act = pl.pallas_call(
        _ffn_a_body,
        grid_spec=pltpu.PrefetchScalarGridSpec(
            num_scalar_prefetch=2, grid=(d_hid // ch, n_blocks),
            in_specs=[pl.BlockSpec((tm, half), lambda c, i, be, nu: (blk(i, nu), 0)),
                      w_spec((d, ch), 2), w_spec((d, ch), 2)],
            out_specs=pl.BlockSpec((tm, ch), lambda c, i, be, nu: (i, c)),
            scratch_shapes=[pltpu.VMEM((d, ch), bf16), pltpu.VMEM((d, ch), bf16)]),
        out_shape=jax.ShapeDtypeStruct((r_rows, d_hid), bf16),
        compiler_params=_cparams("arbitrary", "arbitrary"), name=name + "_up")(block_e, n_used, xp, w1, w3)
    return pl.pallas_call(
        _ffn_b_body,
        grid_spec=pltpu.PrefetchScalarGridSpec(
            num_scalar_prefetch=2, grid=(d // tn, n_blocks),
            in_specs=[pl.BlockSpec((tm, d_hid), lambda c, i, be, nu: (blk(i, nu), 0)),
                      w_spec((d_hid, tn), 2)],
            out_specs=pl.BlockSpec((tm, tn), lambda c, i, be, nu: (i, c)),
            scratch_shapes=[pltpu.VMEM((d_hid, tn), bf16)]),
        out_shape=jax.ShapeDtypeStruct((r_rows, d), f32),
        compiler_params=_cparams("arbitrary", "arbitrary"), name=name + "_down")(block_e, n_used, act, w2)


def _combine_body(pos_ref, y_ref, x1_ref, sh_ref, w_ref, g2_ref, lg_ref, lb_ref, o_ref, buf, sem, *, tc, alpha):
    base = pl.program_id(0) * (TOP_K * tc)

    def row_copy(k, t, src_row):
        return pltpu.make_async_copy(y_ref.at[pl.ds(src_row, 1), :], buf.at[k, pl.ds(t, 1), :], sem)

    for k in range(TOP_K):
        def start(t, carry, k=k):
            row_copy(k, t, pos_ref[base + k * tc + t]).start(priority=k % 2)
            return carry
        lax.fori_loop(0, tc, start, 0)
    for k in range(TOP_K):
        def wait(t, carry, k=k):
            row_copy(k, t, 0).wait()
            return carry
        lax.fori_loop(0, tc, wait, 0)

    w = w_ref[...]
    ffn = buf[0] * w[:, 0:1]
    for k in range(1, TOP_K):
        ffn = ffn + buf[k] * w[:, k:k + 1]
    ffn = ffn + sh_ref[...]
    o_ref[...] = _layer_norm(alpha * x1_ref[...] + g2_ref[...] * ffn, lg_ref[...], lb_ref[...])


def _combine(y_sorted, pos, topw, x1, shared, grp, p, l, *, name):
    dd = _derived()
    d = D_MODEL
    rows = grp.rows
    tc = _tile(rows, T_COMBINE)
    n_t = rows // tc
    pos_flat = pos.reshape(TOP_K, n_t, tc).transpose(1, 0, 2).reshape(-1)
    tiles = grp.rows_per_seq // tc if not grp.per_row else None

    def mod_spec(which):
        if grp.per_row:
            return pl.BlockSpec((tc, d), lambda i, pr: (i, which))
        return pl.BlockSpec((None, 1, d), lambda i, pr: (i // tiles, 0, which))

    row = pl.BlockSpec((tc, d), lambda i, pr: (i, 0))
    par = pl.BlockSpec((None, 1, d), lambda i, pr: (l, 0, 0))
    return pl.pallas_call(
        functools.partial(_combine_body, tc=tc, alpha=dd["dn_alpha"]),
        grid_spec=pltpu.PrefetchScalarGridSpec(
            num_scalar_prefetch=1, grid=(n_t,),
            in_specs=[pl.BlockSpec(memory_space=pl.ANY), row, row,
                      pl.BlockSpec((tc, TOP_K), lambda i, pr: (i, 0)),
                      mod_spec(5), par, par],
            out_specs=row,
            scratch_shapes=[pltpu.VMEM((TOP_K, tc, d), f32), pltpu.SemaphoreType.DMA(())]),
        out_shape=jax.ShapeDtypeStruct((rows, d), f32),
        compiler_params=_cparams("arbitrary"), name=name,
    )(pos_flat, y_sorted, x1, shared, topw.T, grp.mod, p["ln2_g"].reshape(DEPTH, 1, d),
      p["ln2_b"].reshape(DEPTH, 1, d))


def _routing_tables(topi, rank, counts, tm):
    n_tok = topi.shape[1]
    m = n_tok * TOP_K
    n_blocks = -(-m // tm) + N_EXPERTS
    padded = (counts + tm - 1) // tm * tm
    ends = jnp.cumsum(padded)
    starts = ends - padded
    experts = jnp.arange(N_EXPERTS, dtype=jnp.int32)
    start_of = jnp.sum(jnp.where(topi[:, :, None] == experts, starts, 0), axis=-1)
    pos = start_of + rank
    tok = jnp.broadcast_to(jnp.arange(n_tok, dtype=jnp.int32)[None, :], pos.shape)
    slot_tok = jnp.zeros((n_blocks * tm,), jnp.int32).at[pos.reshape(-1)].set(tok.reshape(-1))
    block_e = jnp.sum(ends[None, :] <= (jnp.arange(n_blocks, dtype=jnp.int32) * tm)[:, None], axis=1)
    block_e = jnp.minimum(block_e, N_EXPERTS - 1).astype(jnp.int32)
    n_used = (ends[-1] // tm).astype(jnp.int32).reshape(1)
    return pos, slot_tok, block_e, n_used


def _token_mixer(x, grp, p, l, biases, *, tag, prompt, n_seq, t_seq, conv_prev8, h0, caches, t_new):
    h = _modulate(x, grp, name=f"modulate_{tag}")
    z = _matmul(h, p["w_in"], l, name=f"in_proj_{tag}")
    hr, h_last = _rglru(z, conv_prev8, h0, p, l, n_seq=n_seq, t_seq=t_seq,
                        t_last=(t_seq - 1) % _tile(t_seq, T_SCAN) if prompt else t_new - 1,
                        name=f"rglru_{tag}")
    if prompt:
        att = _attn_prompt(z, biases, n_seq=n_seq, s_len=t_seq, name=f"attn_{tag}")
    else:
        att = _attn_sample(z.reshape(n_seq, SAMPLE_ROWS, z.shape[1]), caches, biases, l,
                           n_seq=n_seq, t_new=t_new, name=f"attn_{tag}")
    merged = _merge(hr, att, z, p, l, name=f"merge_{tag}")
    mix = _matmul(merged, p["w_o"], l, name=f"out_proj_{tag}")
    return mix, z, h_last


def _layer(l, xp, xs, gp, gs, p, biases, caches, state_h, state_conv, dims):
    dd = _derived()
    n_p, s_len, n_s, t_new = dims
    d = D_MODEL
    d_rnn, gw, qkv = dd["d_rnn"], dd["gw"], dd["qkv"]

    zeros_prev = jnp.zeros((n_p, SUBLANE, d_rnn), f32)
    mix_p, z_p, hl_p = _token_mixer(xp, gp, p, l, biases[0], tag="p", prompt=True, n_seq=n_p, t_seq=s_len,
                                    conv_prev8=zeros_prev, h0=jnp.zeros((n_p, d_rnn), f32),
                                    caches=None, t_new=None)
    prev_s = jnp.pad(state_conv[:, l], ((0, 0), (SUBLANE - (CONV_W - 1), 0), (0, 0)))
    mix_s, z_s, hl_s = _token_mixer(xs, gs, p, l, biases[1], tag="s", prompt=False, n_seq=n_s,
                                    t_seq=SAMPLE_ROWS, conv_prev8=prev_s, h0=state_h[:, l],
                                    caches=caches, t_new=t_new)

    x1_p, hp_p, ti_p, tw_p, rk_p, cnt_p = _ln_router(xp, mix_p, gp, p, l, jnp.zeros((N_EXPERTS, LANE), f32),
                                                     name="ln_router_p")
    x1_s, hp_s, ti_s, tw_s, rk_s, cnt = _ln_router(xs, mix_s, gs, p, l, cnt_p, name="ln_router_s")
    topi = jnp.concatenate([ti_p, ti_s], axis=1)
    topw = jnp.concatenate([tw_p, tw_s], axis=1)
    rank = jnp.concatenate([rk_p, rk_s], axis=1)
    counts = cnt[:, 0].astype(jnp.int32)
    pos, slot_tok, block_e, n_used = _routing_tables(topi, rank, counts, TM_EXPERT)

    hp_all = jnp.concatenate([hp_p, hp_s], axis=0)
    x_sorted = _gather_rows(hp_all, slot_tok, name="moe_gather")
    y_sorted = _expert_ffn(x_sorted, p["w_e1"], p["w_e3"], p["w_e2"], (l,), block_e, n_used,
                           tm=TM_EXPERT, name="moe_experts")

    def shared(hp, tag):
        rows = hp.shape[0]
        tm = _tile(rows, TM_EXPERT)
        nb = rows // tm
        return _expert_ffn(hp, p["w_s1"], p["w_s3"], p["w_s2"], (), jnp.full((nb,), l, jnp.int32),
                           jnp.full((1,), nb, jnp.int32), tm=tm, name=f"shared_{tag}")

    n_tok_p = gp.rows
    x2_p = _combine(y_sorted, pos[:, :n_tok_p], topw[:, :n_tok_p], x1_p, shared(hp_p, "p"), gp, p, l,
                    name="combine_p")
    x2_s = _combine(y_sorted, pos[:, n_tok_p:], topw[:, n_tok_p:], x1_s, shared(hp_s, "s"), gs, p, l,
                    name="combine_s")

    z_p3 = z_p.reshape(n_p, s_len, -1)
    z_s3 = z_s.reshape(n_s, SAMPLE_ROWS, -1)
    kv_p, kv_s = [], []
    for g, (win, _) in enumerate(ATT_GROUPS):
        keep = min(win, s_len)
        k0 = d_rnn + qkv + g * gw
        v0 = d_rnn + 2 * qkv + g * gw

        def heads(zz, c0, rows):
            return zz[:, rows, c0:c0 + gw].reshape(zz.shape[0], -1, HEADS_PER_GROUP, HEAD_DIM)

        rp = slice(s_len - keep, s_len)
        kv_p.append(jnp.stack([heads(z_p3, k0, rp), heads(z_p3, v0, rp)], axis=1))
        rs = slice(0, t_new)
        kv_s.append(jnp.stack([heads(z_s3, k0, rs), heads(z_s3, v0, rs)], axis=1))
    conv_p = z_p3[:, s_len - (CONV_W - 1):, :d_rnn]
    conv_s = z_s3[:, t_new - (CONV_W - 1):t_new, :d_rnn] if t_new >= CONV_W - 1 else None
    return x2_p, x2_s, kv_p, kv_s, hl_p, hl_s, conv_p, conv_s


def kernel(x_prompt, x_sample, c_prompt, c_sample, cache_kv_w128, cache_kv_w512, cache_kv_w2048,
           state_rglru_h, state_conv, rel_bias, w_mod, b_mod, w_in, conv_w, conv_b, lru_wa, lru_ba,
           lru_wx, lru_bx, lru_lam, w_pa, w_pb, w_o, ln1_g, ln1_b, w_router, b_router, w_e1, w_e3,
           w_e2, w_s1, w_s3, w_s2, ln2_g, ln2_b):
    p = dict(w_in=w_in, conv_w=conv_w, conv_b=conv_b, lru_wa=lru_wa, lru_ba=lru_ba, lru_wx=lru_wx,
             lru_bx=lru_bx, lru_lam=lru_lam, w_pa=w_pa, w_pb=w_pb, w_o=w_o, ln1_g=ln1_g, ln1_b=ln1_b,
             w_router=w_router, b_router=b_router, w_e1=w_e1, w_e3=w_e3, w_e2=w_e2, w_s1=w_s1,
             w_s3=w_s3, w_s2=w_s2, ln2_g=ln2_g, ln2_b=ln2_b)
    caches = (cache_kv_w128, cache_kv_w512, cache_kv_w2048)
    n_p, s_len, d = x_prompt.shape
    n_s, t_new, _ = x_sample.shape
    assert d == D_MODEL and t_new <= SAMPLE_ROWS and t_new >= CONV_W - 1
    dims = (n_p, s_len, n_s, t_new)

    n_c = n_p + n_s
    c_all = jnp.pad(jnp.concatenate([c_prompt, c_sample], axis=0), ((0, -n_c % SUBLANE), (0, 0)))

    xp = x_prompt.reshape(n_p * s_len, d)
    xs = jnp.pad(x_sample, ((0, 0), (0, SAMPLE_ROWS - t_new), (0, 0))).reshape(n_s * SAMPLE_ROWS, d)
    outs = dict(kv_p=[], kv_s=[], h_p=[], h_s=[], conv_p=[], conv_s=[])
    biases = ([_prompt_bias(rel_bias, g, win // dil, dil) for g, (win, dil) in enumerate(ATT_GROUPS)],
              [_sample_bias(rel_bias, g, win, dil, t_new, caches[g].shape[3])
               for g, (win, dil) in enumerate(ATT_GROUPS)])
    for l in range(DEPTH):
        mod = _matmul(c_all, w_mod, l, name="adaln_mod", bias=b_mod, silu_a=True)
        gp = _Group(n_p * s_len, _tile(s_len, TM_TOKEN), mod[:n_p].reshape(n_p, 1, 6 * d), False, s_len)
        gs = _Group(n_s * SAMPLE_ROWS, n_s * SAMPLE_ROWS, jnp.repeat(mod[n_p:n_c], SAMPLE_ROWS, axis=0),
                    True, SAMPLE_ROWS)
        xp, xs, kv_p, kv_s, hl_p, hl_s, conv_p, conv_s = _layer(
            l, xp, xs, gp, gs, p, biases, caches, state_rglru_h, state_conv, dims)
        for key, val in zip(("kv_p", "kv_s", "h_p", "h_s", "conv_p", "conv_s"),
                            (kv_p, kv_s, hl_p, hl_s, conv_p, conv_s)):
            outs[key].append(val)

    def stack_kv(per_layer, g):
        return jnp.stack([r[g] for r in per_layer], axis=1)

    y_prompt = xp.reshape(n_p, s_len, d)
    y_sample = xs.reshape(n_s, SAMPLE_ROWS, d)[:, :t_new]
    return (y_prompt, y_sample,
            stack_kv(outs["kv_p"], 0), stack_kv(outs["kv_p"], 1), stack_kv(outs["kv_p"], 2),
            jnp.stack(outs["h_p"], axis=1), jnp.stack(outs["conv_p"], axis=1),
            stack_kv(outs["kv_s"], 0), stack_kv(outs["kv_s"], 1), stack_kv(outs["kv_s"], 2),
            jnp.stack(outs["h_s"], axis=1), jnp.stack(outs["conv_s"], axis=1))
```

```python
import functools
import math

import numpy as np
import jax
import jax.numpy as jnp
from jax import lax
from jax.experimental import pallas as pl
from jax.experimental.pallas import tpu as pltpu

bf16 = jnp.bfloat16
f32 = jnp.float32

D_MODEL = 4096
DEPTH = 2
LRU_BLOCKS = 16
CONV_W = 4
LRU_C = 8.0
HEAD_DIM = 128
HEADS_PER_GROUP = 8
ATT_GROUPS = ((128, 1), (512, 4), (2048, 16))
REL_BUCKETS = 32
REL_MAX_DIST = 2048
N_EXPERTS = 64
N_EXPERT_GROUPS = 8
TOPK_GROUPS = 4
TOP_K = 8
ROUTED_SCALE = 2.5
LN_EPS = 1e-5

V7X_VMEM_LIMIT_BYTES = 56 * 1024 * 1024
LANE = 128
SUBLANE = 8
SAMPLE_ROWS = 8
TM_MATMUL = 1024
TN_MATMUL = 512
TM_MERGE = 1024
TN_MERGE = 256
TM_TOKEN = 256
T_SCAN = 256
TM_EXPERT = 256
CH_EXPERT = 512
TN_EXPERT = 2048
T_GATHER = 512
T_COMBINE = 128
DMA_ISSUE_UNROLL = 8


def _derived():
    d_rnn = D_MODEL
    gw = HEADS_PER_GROUP * HEAD_DIM
    qkv = len(ATT_GROUPS) * gw
    return dict(d_rnn=d_rnn, gw=gw, qkv=qkv, in_cols=d_rnn + 3 * qkv + 2 * D_MODEL,
                dn_alpha=(2 * DEPTH) ** 0.25, att_scale=HEAD_DIM ** -0.5,
                lru_block=d_rnn // LRU_BLOCKS)


def _cparams(*sem):
    return pltpu.CompilerParams(dimension_semantics=sem, vmem_limit_bytes=V7X_VMEM_LIMIT_BYTES)


def _tile(n, pref):
    return pref if n % pref == 0 else n


def _nt_dot(a, b):
    return lax.dot_general(a, b, (((1,), (1,)), ((), ())), preferred_element_type=f32)


def _mm_body(*refs, silu_a, has_bias):
    if has_bias:
        a_ref, b_ref, bias_ref, o_ref = refs
    else:
        a_ref, b_ref, o_ref = refs
    a = a_ref[...]
    if silu_a:
        a = a * jax.nn.sigmoid(a)
    acc = jnp.dot(a.astype(bf16), b_ref[...].astype(bf16), preferred_element_type=f32)
    if has_bias:
        acc = acc + bias_ref[...]
    o_ref[...] = acc.astype(o_ref.dtype)


def _matmul(a, w, l, *, name, bias=None, silu_a=False, out_dtype=f32):
    m, k = a.shape
    n = w.shape[-1]
    tm, tn = _tile(m, TM_MATMUL), _tile(n, TN_MATMUL)
    in_specs = [pl.BlockSpec((tm, k), lambda i, j: (i, 0)),
                pl.BlockSpec((None, k, tn), lambda i, j: (l, 0, j))]
    args = [a, w]
    if bias is not None:
        in_specs.append(pl.BlockSpec((None, 1, tn), lambda i, j: (l, 0, j)))
        args.append(bias.reshape(bias.shape[0], 1, n))
    return pl.pallas_call(
        functools.partial(_mm_body, silu_a=silu_a, has_bias=bias is not None),
        grid=(m // tm, n // tn), in_specs=in_specs,
        out_specs=pl.BlockSpec((tm, tn), lambda i, j: (i, j)),
        out_shape=jax.ShapeDtypeStruct((m, n), out_dtype),
        compiler_params=_cparams("parallel", "arbitrary"), name=name)(*args)


class _Group:
    def __init__(self, rows, tm, mod, per_row, rows_per_seq):
        self.rows, self.tm, self.mod, self.per_row, self.rows_per_seq = rows, tm, mod, per_row, rows_per_seq

    def mod_spec(self, which):
        d = D_MODEL
        if self.per_row:
            return pl.BlockSpec((self.tm, d), lambda i: (i, which))
        tiles = self.rows_per_seq // self.tm
        return pl.BlockSpec((None, 1, d), lambda i: (i // tiles, 0, which))

    def row_spec(self, width):
        return pl.BlockSpec((self.tm, width), lambda i: (i, 0))


def _param_spec(l, width):
    return pl.BlockSpec((None, 1, width), lambda i: (l, 0, 0))


def _modulate_body(x_ref, sc_ref, sh_ref, o_ref):
    o_ref[...] = (x_ref[...] * (1.0 + sc_ref[...]) + sh_ref[...]).astype(o_ref.dtype)


def _modulate(x, grp, *, name):
    d = D_MODEL
    return pl.pallas_call(
        _modulate_body, grid=(grp.rows // grp.tm,),
        in_specs=[grp.row_spec(d), grp.mod_spec(1), grp.mod_spec(0)],
        out_specs=grp.row_spec(d), out_shape=jax.ShapeDtypeStruct((grp.rows, d), bf16),
        compiler_params=_cparams("parallel"), name=name)(x, grp.mod, grp.mod)


def _log_sigmoid(x):
    return -(jnp.maximum(-x, 0.0) + jnp.log1p(jnp.exp(-jnp.abs(x))))


def _neg_expm1(y):
    t = jnp.tanh(0.5 * y)
    return -2.0 * t / (1.0 - t)


def _rglru_body(x_ref, cp_ref, h0_ref, cw_ref, cb_ref, wa_ref, ba_ref, wx_ref, bx_ref, lam_ref,
                hr_ref, hl_ref, prev_scr, h_scr, *, tt, t_last):
    i = pl.program_id(2)

    @pl.when(i == 0)
    def _():
        if tt > SUBLANE:
            prev_scr[0:tt - SUBLANE, :] = jnp.zeros((tt - SUBLANE, prev_scr.shape[1]), f32)
        prev_scr[tt - SUBLANE:tt, :] = cp_ref[...]
        h_scr[...] = h0_ref[...]

    x = x_ref[...]
    prev = prev_scr[...]
    row = lax.broadcasted_iota(jnp.int32, x.shape, 0)

    def shifted(k):
        return jnp.where(row < k, pltpu.roll(prev, k, 0), pltpu.roll(x, k, 0))

    cw = cw_ref[...]
    xc = cb_ref[...] + shifted(3) * cw[0:1]
    xc = xc + shifted(2) * cw[1:2]
    xc = xc + shifted(1) * cw[2:3]
    xc = xc + x * cw[3:4]
    prev_scr[...] = x

    xcb = xc.astype(bf16)
    r = jax.nn.sigmoid(jnp.dot(xcb, wa_ref[...].astype(bf16), preferred_element_type=f32) + ba_ref[...])
    gi = jax.nn.sigmoid(jnp.dot(xcb, wx_ref[...].astype(bf16), preferred_element_type=f32) + bx_ref[...])
    log_a = LRU_C * r * _log_sigmoid(lam_ref[...])
    a = jnp.exp(log_a)
    b = jnp.sqrt(_neg_expm1(2.0 * log_a)) * (gi * xc)

    s = 1
    while s < tt:
        a_sh = jnp.where(row < s, 1.0, pltpu.roll(a, s, 0))
        b_sh = jnp.where(row < s, 0.0, pltpu.roll(b, s, 0))
        b = a * b_sh + b
        a = a * a_sh
        s *= 2
    h = a * h_scr[...] + b
    hr_ref[...] = h.astype(hr_ref.dtype)
    h_scr[...] = h[tt - 1:tt, :]
    hl_ref[...] = h[t_last:t_last + 1, :]


def _rglru(z, conv_prev8, h0, p, l, *, n_seq, t_seq, t_last, name):
    dd = _derived()
    lb = dd["lru_block"]
    tt = _tile(t_seq, T_SCAN)
    n_t = t_seq // tt
    d_rnn = dd["d_rnn"]

    def vec(arr):
        return arr.reshape(DEPTH, 1, d_rnn)

    vspec = pl.BlockSpec((None, 1, lb), lambda b, n, i: (l, 0, n))
    wspec = pl.BlockSpec((None, None, lb, lb), lambda b, n, i: (l, n, 0, 0))
    hr, h_last = pl.pallas_call(
        functools.partial(_rglru_body, tt=tt, t_last=t_last),
        grid=(n_seq, LRU_BLOCKS, n_t),
        in_specs=[pl.BlockSpec((tt, lb), lambda b, n, i: (b * n_t + i, n)),
                  pl.BlockSpec((None, SUBLANE, lb), lambda b, n, i: (b, 0, n)),
                  pl.BlockSpec((None, 1, lb), lambda b, n, i: (b, 0, n)),
                  pl.BlockSpec((None, CONV_W, lb), lambda b, n, i: (l, 0, n)),
                  vspec, wspec, vspec, wspec, vspec, vspec],
        out_specs=[pl.BlockSpec((tt, lb), lambda b, n, i: (b * n_t + i, n)),
                   pl.BlockSpec((None, 1, lb), lambda b, n, i: (b, 0, n))],
        out_shape=[jax.ShapeDtypeStruct((n_seq * t_seq, d_rnn), bf16),
                   jax.ShapeDtypeStruct((n_seq, 1, d_rnn), f32)],
        scratch_shapes=[pltpu.VMEM((tt, lb), f32), pltpu.VMEM((1, lb), f32)],
        compiler_params=_cparams("parallel", "parallel", "arbitrary"), name=name,
    )(z, conv_prev8, h0.reshape(n_seq, 1, d_rnn), p["conv_w"], vec(p["conv_b"]), p["lru_wa"],
      vec(p["lru_ba"]), p["lru_wx"], vec(p["lru_bx"]), vec(p["lru_lam"]))
    return hr, h_last.reshape(n_seq, d_rnn)


def _rel_bucket(dist):
    dist = np.asarray(dist, np.int32)
    max_exact = REL_BUCKETS // 2
    df = np.maximum(dist, 1).astype(np.float32)
    large = max_exact + (np.log(df / np.float32(max_exact)) / np.float32(math.log(REL_MAX_DIST / max_exact))
                         * np.float32(REL_BUCKETS - max_exact)).astype(np.int32)
    large = np.minimum(large, REL_BUCKETS - 1)
    return np.where(dist < max_exact, dist, large)


def _bias_lookup(tab, bucket, ok):
    flat = jnp.asarray(bucket.reshape(-1), jnp.int32)
    onehot = (flat[:, None] == jnp.arange(REL_BUCKETS, dtype=jnp.int32)[None, :]).astype(f32)
    vals = jnp.dot(onehot, tab.astype(f32), precision=lax.Precision.HIGHEST)
    vals = vals.reshape(bucket.shape + (tab.shape[1],))
    return jnp.where(jnp.asarray(ok)[..., None], vals, -jnp.inf)


def _prompt_bias(rel_bias, g, band, dil):
    rel = np.arange(band)[:, None] + band - np.arange(2 * band)[None, :]
    in_band = (rel >= 0) & (rel <= band)
    tab = rel_bias[:, g * HEADS_PER_GROUP:(g + 1) * HEADS_PER_GROUP]
    vals = _bias_lookup(tab, _rel_bucket(np.clip(rel, 0, band) * dil), in_band)
    return vals.transpose(2, 0, 1)


def _sample_bias(rel_bias, g, win, dil, t_new, lb_cache):
    nk = win // dil + 1
    t = np.arange(SAMPLE_ROWS)[:, None]
    tab = rel_bias[:, g * HEADS_PER_GROUP:(g + 1) * HEADS_PER_GROUP]

    def table(delta):
        ok = (delta >= 0) & (delta % dil == 0) & (delta // dil < nk)
        vals = _bias_lookup(tab, _rel_bucket(np.maximum(delta, 0)), ok)
        vals = jnp.where(jnp.asarray(t < t_new)[:, :, None], vals, 0.0)
        return vals.transpose(2, 0, 1)

    cache = table(lb_cache + t - np.arange(lb_cache)[None, :])
    j = np.arange(SAMPLE_ROWS)[None, :]
    new = table(np.where(j < t_new, t - j, -1))
    return cache, new


def _attn_prompt_body(*refs, s_len, scale):
    n_g = len(ATT_GROUPS)
    o_ref, o_scr, l_scr = refs[4 * n_g:]
    for g, (win, dil) in enumerate(ATT_GROUPS):
        q_ref, k_ref, v_ref, b_ref = refs[4 * g:4 * g + 4]
        band = win // dil
        nb = (s_len // dil) // band
        bias = b_ref[...]
        for r in range(dil):
            for blk in range(nb):
                def rows(first_blk, n_rows):
                    start = r + first_blk * band * dil
                    return pl.ds(start, n_rows) if dil == 1 else pl.ds(start, n_rows, stride=dil)
                qrows = rows(blk, band)
                krows, bb = (qrows, bias[:, band:]) if blk == 0 else (rows(blk - 1, 2 * band), bias)
                q = q_ref[qrows, :].astype(bf16)
                k = k_ref[krows, :].astype(bf16)
                v = v_ref[krows, :].astype(bf16)
                s = _nt_dot(q, k) * scale + bb
                mx = jnp.max(s, axis=-1, keepdims=True)
                p = jnp.exp(s - mx)
                den = jnp.sum(p, axis=-1, keepdims=True)
                o = jnp.dot((p / den).astype(bf16), v, preferred_element_type=f32)
                o_scr[g, qrows, :] = o
                l_scr[g, qrows, :] = jnp.broadcast_to(mx + jnp.log(den), o.shape)
    ch = 256
    for c in range(s_len // ch):
        sl = pl.ds(c * ch, ch)
        ls = [l_scr[g, sl, :] for g in range(n_g)]
        m = functools.reduce(jnp.maximum, ls)
        es = [jnp.exp(x - m) for x in ls]
        tot = functools.reduce(lambda a, b: a + b, es)
        out = functools.reduce(lambda a, b: a + b, [(es[g] / tot) * o_scr[g, sl, :] for g in range(n_g)])
        o_ref[sl, :] = out.astype(o_ref.dtype)


def _attn_prompt(z, biases, *, n_seq, s_len, name):
    dd = _derived()
    gw, d_rnn, qkv = dd["gw"], dd["d_rnn"], dd["qkv"]
    in_specs, args = [], []
    for g, (win, dil) in enumerate(ATT_GROUPS):
        band = win // dil
        assert (s_len // dil) % band == 0
        for part in range(3):
            c0 = (d_rnn + part * qkv + g * gw) // HEAD_DIM
            in_specs.append(pl.BlockSpec((s_len, HEAD_DIM), lambda b, h, c0=c0: (b, c0 + h)))
            args.append(z)
        in_specs.append(pl.BlockSpec((None, band, 2 * band), lambda b, h: (h, 0, 0)))
        args.append(biases[g])
    n_g = len(ATT_GROUPS)
    return pl.pallas_call(
        functools.partial(_attn_prompt_body, s_len=s_len, scale=dd["att_scale"]),
        grid=(n_seq, HEADS_PER_GROUP), in_specs=in_specs,
        out_specs=pl.BlockSpec((s_len, HEAD_DIM), lambda b, h: (b, h)),
        out_shape=jax.ShapeDtypeStruct((n_seq * s_len, gw), bf16),
        scratch_shapes=[pltpu.VMEM((n_g, s_len, HEAD_DIM), f32), pltpu.VMEM((n_g, s_len, HEAD_DIM), f32)],
        compiler_params=_cparams("parallel", "parallel"), name=name)(*args)


def _attn_sample_body(*refs, scale):
    n_g = len(ATT_GROUPS)
    o_ref = refs[7 * n_g]
    outs, lses = [], []
    for g in range(n_g):
        q_ref, kn_ref, vn_ref, kc_ref, vc_ref, bc_ref, bn_ref = refs[7 * g:7 * g + 7]
        q = q_ref[...].astype(bf16)
        s_c = _nt_dot(q, kc_ref[...].astype(bf16)) * scale + bc_ref[...]
        s_n = _nt_dot(q, kn_ref[...].astype(bf16)) * scale + bn_ref[...]
        mx = jnp.maximum(jnp.max(s_c, axis=-1, keepdims=True), jnp.max(s_n, axis=-1, keepdims=True))
        p_c = jnp.exp(s_c - mx)
        p_n = jnp.exp(s_n - mx)
        den = jnp.sum(p_c, axis=-1, keepdims=True) + jnp.sum(p_n, axis=-1, keepdims=True)
        o = jnp.dot((p_c / den).astype(bf16), vc_ref[...].astype(bf16), preferred_element_type=f32)
        o = o + jnp.dot((p_n / den).astype(bf16), vn_ref[...].astype(bf16), preferred_element_type=f32)
        outs.append(o)
        lses.append(mx + jnp.log(den))
    m = functools.reduce(jnp.maximum, lses)
    es = [jnp.exp(x - m) for x in lses]
    tot = functools.reduce(lambda a, b: a + b, es)
    o_ref[...] = functools.reduce(lambda a, b: a + b, [(es[g] / tot) * outs[g] for g in range(n_g)])


def _attn_sample(z3, caches, biases, l, *, n_seq, t_new, name):
    dd = _derived()
    gw, d_rnn, qkv = dd["gw"], dd["d_rnn"], dd["qkv"]
    in_specs, args = [], []
    for g, (win, dil) in enumerate(ATT_GROUPS):
        cache = caches[g]
        lb_cache = cache.shape[3]
        cache = cache.reshape(n_seq, DEPTH, 2, lb_cache, gw)
        for part in range(3):
            c0 = (d_rnn + part * qkv + g * gw) // HEAD_DIM
            in_specs.append(pl.BlockSpec((None, SAMPLE_ROWS, HEAD_DIM), lambda b, h, c0=c0: (b, 0, c0 + h)))
            args.append(z3)
        for kv in range(2):
            in_specs.append(pl.BlockSpec((None, None, None, lb_cache, HEAD_DIM),
                                         lambda b, h, kv=kv: (b, l, kv, 0, h)))
            args.append(cache)
        b_cache, b_new = biases[g]
        in_specs.append(pl.BlockSpec((None, SAMPLE_ROWS, lb_cache), lambda b, h: (h, 0, 0)))
        in_specs.append(pl.BlockSpec((None, SAMPLE_ROWS, SAMPLE_ROWS), lambda b, h: (h, 0, 0)))
        args += [b_cache, b_new]
    out = pl.pallas_call(
        functools.partial(_attn_sample_body, scale=dd["att_scale"]),
        grid=(n_seq, HEADS_PER_GROUP), in_specs=in_specs,
        out_specs=pl.BlockSpec((None, SAMPLE_ROWS, HEAD_DIM), lambda b, h: (b, 0, h)),
        out_shape=jax.ShapeDtypeStruct((n_seq, SAMPLE_ROWS, gw), f32),
        compiler_params=_cparams("parallel", "parallel"), name=name)(*args)
    return out.reshape(n_seq * SAMPLE_ROWS, gw)


def _merge_body(hr_ref, att_ref, wa_ref, wb_ref, ga_ref, gb_ref, o_ref):
    pa = jnp.dot(hr_ref[...].astype(bf16), wa_ref[...].astype(bf16), preferred_element_type=f32)
    pb = jnp.dot(att_ref[...].astype(bf16), wb_ref[...].astype(bf16), preferred_element_type=f32)
    o_ref[...] = (jax.nn.sigmoid(ga_ref[...]) * pa + jax.nn.sigmoid(gb_ref[...]) * pb).astype(o_ref.dtype)


def _merge(hr, att, z, p, l, *, name):
    dd = _derived()
    m = hr.shape[0]
    d = D_MODEL
    tm, tn = _tile(m, TM_MERGE), _tile(d, TN_MERGE)
    ga0 = (dd["d_rnn"] + 3 * dd["qkv"]) // tn
    gb0 = ga0 + d // tn
    return pl.pallas_call(
        _merge_body, grid=(m // tm, d // tn),
        in_specs=[pl.BlockSpec((tm, hr.shape[1]), lambda i, j: (i, 0)),
                  pl.BlockSpec((tm, att.shape[1]), lambda i, j: (i, 0)),
                  pl.BlockSpec((None, hr.shape[1], tn), lambda i, j: (l, 0, j)),
                  pl.BlockSpec((None, att.shape[1], tn), lambda i, j: (l, 0, j)),
                  pl.BlockSpec((tm, tn), lambda i, j: (i, ga0 + j)),
                  pl.BlockSpec((tm, tn), lambda i, j: (i, gb0 + j))],
        out_specs=pl.BlockSpec((tm, tn), lambda i, j: (i, j)),
        out_shape=jax.ShapeDtypeStruct((m, d), bf16),
        compiler_params=_cparams("parallel", "arbitrary"), name=name)(hr, att, p["w_pa"], p["w_pb"], z, z)


def _layer_norm(v, g, b):
    mu = jnp.mean(v, axis=-1, keepdims=True)
    var = jnp.mean(jnp.square(v - mu), axis=-1, keepdims=True)
    return (v - mu) * lax.rsqrt(var + LN_EPS) * g + b


def _pack_halves(h):
    half = h.shape[1] // 2
    lo = lax.bitcast_convert_type(h[:, :half].astype(bf16).astype(f32), jnp.uint32)
    hi = lax.bitcast_convert_type(h[:, half:].astype(bf16).astype(f32), jnp.uint32)
    return (lo >> 16) | (hi & jnp.uint32(0xFFFF0000))


def _unpack_halves(xu):
    lo = lax.bitcast_convert_type(xu << 16, f32).astype(bf16)
    hi = lax.bitcast_convert_type(xu & jnp.uint32(0xFFFF0000), f32).astype(bf16)
    return lo, hi


def _split_bf16(x):
    hi = x.astype(bf16)
    return hi, (x - hi.astype(f32)).astype(bf16)


def _first_max(vals, idx):
    m = functools.reduce(jnp.maximum, [jnp.max(v, axis=0, keepdims=True) for v in vals])
    big = float(N_EXPERTS)
    cand = [jnp.min(jnp.where(v == m, i, big), axis=0, keepdims=True) for v, i in zip(vals, idx)]
    return m, functools.reduce(jnp.minimum, cand)


def _ln_router_body(x_ref, mix_ref, g1_ref, sc_ref, sh_ref, lg_ref, lb_ref, wr_ref, br_ref, base_ref,
                    x1_ref, hp_ref, topi_ref, topw_ref, rank_ref, cnt_ref, *, alpha):
    i = pl.program_id(0)
    x1 = _layer_norm(alpha * x_ref[...] + g1_ref[...] * mix_ref[...], lg_ref[...], lb_ref[...])
    x1_ref[...] = x1
    h = x1 * (1.0 + sc_ref[...]) + sh_ref[...]
    hp_ref[...] = _pack_halves(h)

    h_hi, h_lo = _split_bf16(h)
    w_hi, w_lo = _split_bf16(wr_ref[...])
    logits = _nt_dot(w_hi, h_hi) + _nt_dot(w_hi, h_lo) + _nt_dot(w_lo, h_hi)
    s = jax.nn.sigmoid(logits)
    sel = s + br_ref[...]
    tm = s.shape[1]
    per = N_EXPERTS // N_EXPERT_GROUPS
    sub = lax.broadcasted_iota(jnp.int32, (per, tm), 0).astype(f32)
    sel_g = [sel[g * per:(g + 1) * per, :] for g in range(N_EXPERT_GROUPS)]
    s_g = [s[g * per:(g + 1) * per, :] for g in range(N_EXPERT_GROUPS)]
    idx_g = [sub + float(g * per) for g in range(N_EXPERT_GROUPS)]
    neg = -jnp.inf

    grp = []
    for g in range(N_EXPERT_GROUPS):
        m1, i1 = _first_max([sel_g[g]], [idx_g[g]])
        m2 = jnp.max(jnp.where(idx_g[g] == i1, neg, sel_g[g]), axis=0, keepdims=True)
        grp.append(m1 + m2)
    keep = [jnp.zeros((1, tm), jnp.bool_) for _ in range(N_EXPERT_GROUPS)]
    for _ in range(TOPK_GROUPS):
        m = functools.reduce(jnp.maximum, grp)
        found = jnp.zeros((1, tm), jnp.bool_)
        for g in range(N_EXPERT_GROUPS):
            hit = (grp[g] == m) & jnp.logical_not(found)
            found = found | hit
            keep[g] = keep[g] | hit
            grp[g] = jnp.where(hit, neg, grp[g])
    cand = [jnp.where(keep[g], sel_g[g], neg) for g in range(N_EXPERT_GROUPS)]

    @pl.when(i == 0)
    def _():
        cnt_ref[...] = base_ref[...]

    picked = [jnp.zeros((per, tm), f32) for _ in range(N_EXPERT_GROUPS)]
    top_i, top_w = [], []
    for _ in range(TOP_K):
        _, ik = _first_max(cand, idx_g)
        hits = [idx_g[g] == ik for g in range(N_EXPERT_GROUPS)]
        wk = functools.reduce(lambda a, b: a + b,
                              [jnp.sum(jnp.where(hits[g], s_g[g], 0.0), axis=0, keepdims=True)
                               for g in range(N_EXPERT_GROUPS)])
        for g in range(N_EXPERT_GROUPS):
            cand[g] = jnp.where(hits[g], neg, cand[g])
            picked[g] = jnp.where(hits[g], 1.0, picked[g])
        top_i.append(ik)
        top_w.append(wk)
    wsum = functools.reduce(lambda a, b: a + b, top_w)

    onehot = jnp.concatenate(picked, axis=0)
    r_i = lax.broadcasted_iota(jnp.int32, (tm, tm), 0)
    c_i = lax.broadcasted_iota(jnp.int32, (tm, tm), 1)
    before = (r_i < c_i).astype(bf16)
    prefix = jnp.dot(onehot.astype(bf16), before, preferred_element_type=f32) + cnt_ref[:, 0:1]
    cnt_ref[...] = cnt_ref[...] + jnp.sum(onehot, axis=1, keepdims=True)
    pre_g = [prefix[g * per:(g + 1) * per, :] for g in range(N_EXPERT_GROUPS)]
    for k in range(TOP_K):
        rk = functools.reduce(lambda a, b: a + b,
                              [jnp.sum(jnp.where(idx_g[g] == top_i[k], pre_g[g], 0.0), axis=0, keepdims=True)
                               for g in range(N_EXPERT_GROUPS)])
        topi_ref[k:k + 1, :] = top_i[k].astype(jnp.int32)
        topw_ref[k:k + 1, :] = top_w[k] / wsum * ROUTED_SCALE
        rank_ref[k:k + 1, :] = rk.astype(jnp.int32)


def _ln_router(x, mix, grp, p, l, base_counts, *, name):
    dd = _derived()
    d = D_MODEL
    rows, tm = grp.rows, grp.tm
    w_rt = jnp.swapaxes(p["w_router"], 1, 2)
    b_r = p["b_router"].reshape(DEPTH, N_EXPERTS, 1)
    tok = lambda dt: jax.ShapeDtypeStruct((TOP_K, rows), dt)
    tok_spec = pl.BlockSpec((TOP_K, tm), lambda i: (0, i))
    return pl.pallas_call(
        functools.partial(_ln_router_body, alpha=dd["dn_alpha"]),
        grid=(rows // tm,),
        in_specs=[grp.row_spec(d), grp.row_spec(d), grp.mod_spec(2), grp.mod_spec(4), grp.mod_spec(3),
                  _param_spec(l, d), _param_spec(l, d),
                  pl.BlockSpec((None, N_EXPERTS, d), lambda i: (l, 0, 0)),
                  pl.BlockSpec((None, N_EXPERTS, 1), lambda i: (l, 0, 0)),
                  pl.BlockSpec((N_EXPERTS, LANE), lambda i: (0, 0))],
        out_specs=[grp.row_spec(d), grp.row_spec(d // 2), tok_spec, tok_spec, tok_spec,
                   pl.BlockSpec((N_EXPERTS, LANE), lambda i: (0, 0))],
        out_shape=[jax.ShapeDtypeStruct((rows, d), f32), jax.ShapeDtypeStruct((rows, d // 2), jnp.uint32),
                   tok(jnp.int32), tok(f32), tok(jnp.int32),
                   jax.ShapeDtypeStruct((N_EXPERTS, LANE), f32)],
        compiler_params=_cparams("arbitrary"), name=name,
    )(x, mix, grp.mod, grp.mod, grp.mod, p["ln1_g"].reshape(DEPTH, 1, d), p["ln1_b"].reshape(DEPTH, 1, d),
      w_rt, b_r, base_counts)


def _gather_body(idx_ref, src_ref, o_ref, sem, *, tg):
    base = pl.program_id(0) * tg

    def row_copy(r, src_row):
        return pltpu.make_async_copy(src_ref.at[pl.ds(src_row, 1), :], o_ref.at[pl.ds(r, 1), :], sem)

    def start(j, carry):
        for q in range(2):
            r = 2 * j + q
            row_copy(r, idx_ref[base + r]).start(priority=q)
        return carry

    lax.fori_loop(0, tg // 2, start, 0, unroll=DMA_ISSUE_UNROLL // 2)
    pltpu.make_async_copy(src_ref.at[pl.ds(0, tg), :], o_ref, sem).wait()


def _gather_rows(src, idx, *, name):
    r_out, width = idx.shape[0], src.shape[1]
    tg = _tile(r_out, T_GATHER)
    assert tg % 2 == 0
    return pl.pallas_call(
        functools.partial(_gather_body, tg=tg),
        grid_spec=pltpu.PrefetchScalarGridSpec(
            num_scalar_prefetch=1, grid=(r_out // tg,),
            in_specs=[pl.BlockSpec(memory_space=pl.ANY)],
            out_specs=pl.BlockSpec((tg, width), lambda i, idx_ref: (i, 0)),
            scratch_shapes=[pltpu.SemaphoreType.DMA(())]),
        out_shape=jax.ShapeDtypeStruct((r_out, width), src.dtype),
        compiler_params=_cparams("arbitrary"), name=name)(idx, src)


def _expert_changed(be_ref, i):
    return (i == 0) | (be_ref[i] != be_ref[jnp.maximum(i - 1, 0)])


def _ffn_a_body(be_ref, nu_ref, x_ref, w1_ref, w3_ref, o_ref, w1_scr, w3_scr):
    i = pl.program_id(1)

    @pl.when(i < nu_ref[0])
    def _():
        @pl.when(_expert_changed(be_ref, i))
        def _():
            w1_scr[...] = w1_ref[...].astype(bf16)
            w3_scr[...] = w3_ref[...].astype(bf16)

        lo, hi = _unpack_halves(x_ref[...])
        half = lo.shape[1]
        h1 = (jnp.dot(lo, w1_scr[0:half, :], preferred_element_type=f32)
              + jnp.dot(hi, w1_scr[half:, :], preferred_element_type=f32))
        h3 = (jnp.dot(lo, w3_scr[0:half, :], preferred_element_type=f32)
              + jnp.dot(hi, w3_scr[half:, :], preferred_element_type=f32))
        o_ref[...] = (h1 * jax.nn.sigmoid(h1) * h3).astype(o_ref.dtype)

    @pl.when(i >= nu_ref[0])
    def _():
        o_ref[...] = jnp.zeros(o_ref.shape, o_ref.dtype)


def _ffn_b_body(be_ref, nu_ref, a_ref, w2_ref, o_ref, w2_scr):
    i = pl.program_id(1)

    @pl.when(i < nu_ref[0])
    def _():
        @pl.when(_expert_changed(be_ref, i))
        def _():
            w2_scr[...] = w2_ref[...].astype(bf16)

        o_ref[...] = jnp.dot(a_ref[...], w2_scr[...], preferred_element_type=f32)

    @pl.when(i >= nu_ref[0])
    def _():
        o_ref[...] = jnp.zeros(o_ref.shape, o_ref.dtype)


def _expert_ffn(xp, w1, w3, w2, lead, block_e, n_used, *, tm, name):
    r_rows, half = xp.shape
    d = 2 * half
    d_hid = w1.shape[-1]
    n_blocks = r_rows // tm
    ch, tn = _tile(d_hid, CH_EXPERT), _tile(d, TN_EXPERT)
    nl = len(lead)

    def blk(i, nu_ref):
        return jnp.minimum(i, nu_ref[0] - 1)

    def w_spec(shape, col_axis):
        def index(c, i, be_ref, nu_ref):
            e = be_ref[blk(i, nu_ref)]
            return lead + ((e, 0, c) if col_axis == 2 else (e, c, 0))
        return pl.BlockSpec((None,) * (nl + 1) + shape, index)

    act = pl.pallas_call(
        _ffn_a_body,
        grid_spec=pltpu.PrefetchScalarGridSpec(
            num_scalar_prefetch=2, grid=(d_hid // ch, n_blocks),
            in_specs=[pl.BlockSpec((tm, half), lambda c, i, be, nu: (blk(i, nu), 0)),
                      w_spec((d, ch), 2), w_spec((d, ch), 2)],
            out_specs=pl.BlockSpec((tm, ch), lambda c, i, be, nu: (i, c)),
            scratch_shapes=[pltpu.VMEM((d, ch), bf16), pltpu.VMEM((d, ch), bf16)]),
        out_shape=jax.ShapeDtypeStruct((r_rows, d_hid), bf16),
        compiler_params=_cparams("arbitrary", "arbitrary"), name=name + "_up")(block_e, n_used, xp, w1, w3)
    return pl.pallas_call(
        _ffn_b_body,
        grid_spec=pltpu.PrefetchScalarGridSpec(
            num_scalar_prefetch=2, grid=(d // tn, n_blocks),
            in_specs=[pl.BlockSpec((tm, d_hid), lambda c, i, be, nu: (blk(i, nu), 0)),
                      w_spec((d_hid, tn), 2)],
            out_specs=pl.BlockSpec((tm, tn), lambda c, i, be, nu: (i, c)),
            scratch_shapes=[pltpu.VMEM((d_hid, tn), bf16)]),
        out_shape=jax.ShapeDtypeStruct((r_rows, d), f32),
        compiler_params=_cparams("arbitrary", "arbitrary"), name=name + "_down")(block_e, n_used, act, w2)


def _combine_body(pos_ref, y_ref, x1_ref, sh_ref, w_ref, g2_ref, lg_ref, lb_ref, o_ref, buf, sem, *, tc, alpha):
    base = pl.program_id(0) * (TOP_K * tc)

    def row_copy(k, t, src_row):
        return pltpu.make_async_copy(y_ref.at[pl.ds(src_row, 1), :], buf.at[k, pl.ds(t, 1), :], sem)

    for k in range(TOP_K):
        def start(t, carry, k=k):
            row_copy(k, t, pos_ref[base + k * tc + t]).start(priority=k % 2)
            return carry
        lax.fori_loop(0, tc, start, 0, unroll=DMA_ISSUE_UNROLL)
    for k in range(TOP_K):
        pltpu.make_async_copy(y_ref.at[pl.ds(0, tc), :], buf.at[k], sem).wait()

    w = w_ref[...]
    ffn = buf[0] * w[:, 0:1]
    for k in range(1, TOP_K):
        ffn = ffn + buf[k] * w[:, k:k + 1]
    ffn = ffn + sh_ref[...]
    o_ref[...] = _layer_norm(alpha * x1_ref[...] + g2_ref[...] * ffn, lg_ref[...], lb_ref[...])


def _combine(y_sorted, pos, topw, x1, shared, grp, p, l, *, name):
    dd = _derived()
    d = D_MODEL
    rows = grp.rows
    tc = _tile(rows, T_COMBINE)
    n_t = rows // tc
    pos_flat = pos.reshape(TOP_K, n_t, tc).transpose(1, 0, 2).reshape(-1)
    tiles = grp.rows_per_seq // tc if not grp.per_row else None

    def mod_spec(which):
        if grp.per_row:
            return pl.BlockSpec((tc, d), lambda i, pr: (i, which))
        return pl.BlockSpec((None, 1, d), lambda i, pr: (i // tiles, 0, which))

    row = pl.BlockSpec((tc, d), lambda i, pr: (i, 0))
    par = pl.BlockSpec((None, 1, d), lambda i, pr: (l, 0, 0))
    return pl.pallas_call(
        functools.partial(_combine_body, tc=tc, alpha=dd["dn_alpha"]),
        grid_spec=pltpu.PrefetchScalarGridSpec(
            num_scalar_prefetch=1, grid=(n_t,),
            in_specs=[pl.BlockSpec(memory_space=pl.ANY), row, row,
                      pl.BlockSpec((tc, TOP_K), lambda i, pr: (i, 0)),
                      mod_spec(5), par, par],
            out_specs=row,
            scratch_shapes=[pltpu.VMEM((TOP_K, tc, d), f32), pltpu.SemaphoreType.DMA(())]),
        out_shape=jax.ShapeDtypeStruct((rows, d), f32),
        compiler_params=_cparams("arbitrary"), name=name,
    )(pos_flat, y_sorted, x1, shared, topw.T, grp.mod, p["ln2_g"].reshape(DEPTH, 1, d),
      p["ln2_b"].reshape(DEPTH, 1, d))


def _routing_tables(topi, rank, counts, tm):
    n_tok = topi.shape[1]
    m = n_tok * TOP_K
    n_blocks = -(-m // tm) + N_EXPERTS
    padded = (counts + tm - 1) // tm * tm
    ends = jnp.cumsum(padded)
    starts = ends - padded
    experts = jnp.arange(N_EXPERTS, dtype=jnp.int32)
    start_of = jnp.sum(jnp.where(topi[:, :, None] == experts, starts, 0), axis=-1)
    pos = start_of + rank
    tok = jnp.broadcast_to(jnp.arange(n_tok, dtype=jnp.int32)[None, :], pos.shape)
    slot_tok = jnp.zeros((n_blocks * tm,), jnp.int32).at[pos.reshape(-1)].set(tok.reshape(-1))
    block_e = jnp.sum(ends[None, :] <= (jnp.arange(n_blocks, dtype=jnp.int32) * tm)[:, None], axis=1)
    block_e = jnp.minimum(block_e, N_EXPERTS - 1).astype(jnp.int32)
    n_used = (ends[-1] // tm).astype(jnp.int32).reshape(1)
    return pos, slot_tok, block_e, n_used


def _token_mixer(x, grp, p, l, biases, *, tag, prompt, n_seq, t_seq, conv_prev8, h0, caches, t_new):
    h = _modulate(x, grp, name=f"modulate_{tag}")
    z = _matmul(h, p["w_in"], l, name=f"in_proj_{tag}")
    hr, h_last = _rglru(z, conv_prev8, h0, p, l, n_seq=n_seq, t_seq=t_seq,
                        t_last=(t_seq - 1) % _tile(t_seq, T_SCAN) if prompt else t_new - 1,
                        name=f"rglru_{tag}")
    if prompt:
        att = _attn_prompt(z, biases, n_seq=n_seq, s_len=t_seq, name=f"attn_{tag}")
    else:
        att = _attn_sample(z.reshape(n_seq, SAMPLE_ROWS, z.shape[1]), caches, biases, l,
                           n_seq=n_seq, t_new=t_new, name=f"attn_{tag}")
    merged = _merge(hr, att, z, p, l, name=f"merge_{tag}")
    mix = _matmul(merged, p["w_o"], l, name=f"out_proj_{tag}")
    return mix, z, h_last


def _layer(l, xp, xs, gp, gs, p, biases, caches, state_h, state_conv, dims):
    dd = _derived()
    n_p, s_len, n_s, t_new = dims
    d = D_MODEL
    d_rnn, gw, qkv = dd["d_rnn"], dd["gw"], dd["qkv"]

    zeros_prev = jnp.zeros((n_p, SUBLANE, d_rnn), f32)
    mix_p, z_p, hl_p = _token_mixer(xp, gp, p, l, biases[0], tag="p", prompt=True, n_seq=n_p, t_seq=s_len,
                                    conv_prev8=zeros_prev, h0=jnp.zeros((n_p, d_rnn), f32),
                                    caches=None, t_new=None)
    prev_s = jnp.pad(state_conv[:, l], ((0, 0), (SUBLANE - (CONV_W - 1), 0), (0, 0)))
    mix_s, z_s, hl_s = _token_mixer(xs, gs, p, l, biases[1], tag="s", prompt=False, n_seq=n_s,
                                    t_seq=SAMPLE_ROWS, conv_prev8=prev_s, h0=state_h[:, l],
                                    caches=caches, t_new=t_new)

    x1_p, hp_p, ti_p, tw_p, rk_p, cnt_p = _ln_router(xp, mix_p, gp, p, l, jnp.zeros((N_EXPERTS, LANE), f32),
                                                     name="ln_router_p")
    x1_s, hp_s, ti_s, tw_s, rk_s, cnt = _ln_router(xs, mix_s, gs, p, l, cnt_p, name="ln_router_s")
    topi = jnp.concatenate([ti_p, ti_s], axis=1)
    topw = jnp.concatenate([tw_p, tw_s], axis=1)
    rank = jnp.concatenate([rk_p, rk_s], axis=1)
    counts = cnt[:, 0].astype(jnp.int32)
    pos, slot_tok, block_e, n_used = _routing_tables(topi, rank, counts, TM_EXPERT)

    hp_all = jnp.concatenate([hp_p, hp_s], axis=0)
    x_sorted = _gather_rows(hp_all, slot_tok, name="moe_gather")
    y_sorted = _expert_ffn(x_sorted, p["w_e1"], p["w_e3"], p["w_e2"], (l,), block_e, n_used,
                           tm=TM_EXPERT, name="moe_experts")

    def shared(hp, tag):
        rows = hp.shape[0]
        tm = _tile(rows, TM_EXPERT)
        nb = rows // tm
        return _expert_ffn(hp, p["w_s1"], p["w_s3"], p["w_s2"], (), jnp.full((nb,), l, jnp.int32),
                           jnp.full((1,), nb, jnp.int32), tm=tm, name=f"shared_{tag}")

    n_tok_p = gp.rows
    x2_p = _combine(y_sorted, pos[:, :n_tok_p], topw[:, :n_tok_p], x1_p, shared(hp_p, "p"), gp, p, l,
                    name="combine_p")
    x2_s = _combine(y_sorted, pos[:, n_tok_p:], topw[:, n_tok_p:], x1_s, shared(hp_s, "s"), gs, p, l,
                    name="combine_s")

    z_p3 = z_p.reshape(n_p, s_len, -1)
    z_s3 = z_s.reshape(n_s, SAMPLE_ROWS, -1)
    kv_p, kv_s = [], []
    for g, (win, _) in enumerate(ATT_GROUPS):
        keep = min(win, s_len)
        k0 = d_rnn + qkv + g * gw
        v0 = d_rnn + 2 * qkv + g * gw

        def heads(zz, c0, rows):
            return zz[:, rows, c0:c0 + gw].reshape(zz.shape[0], -1, HEADS_PER_GROUP, HEAD_DIM)

        rp = slice(s_len - keep, s_len)
        kv_p.append(jnp.stack([heads(z_p3, k0, rp), heads(z_p3, v0, rp)], axis=1))
        rs = slice(0, t_new)
        kv_s.append(jnp.stack([heads(z_s3, k0, rs), heads(z_s3, v0, rs)], axis=1))
    conv_p = z_p3[:, s_len - (CONV_W - 1):, :d_rnn]
    conv_s = z_s3[:, t_new - (CONV_W - 1):t_new, :d_rnn] if t_new >= CONV_W - 1 else None
    return x2_p, x2_s, kv_p, kv_s, hl_p, hl_s, conv_p, conv_s


def kernel(x_prompt, x_sample, c_prompt, c_sample, cache_kv_w128, cache_kv_w512, cache_kv_w2048,
           state_rglru_h, state_conv, rel_bias, w_mod, b_mod, w_in, conv_w, conv_b, lru_wa, lru_ba,
           lru_wx, lru_bx, lru_lam, w_pa, w_pb, w_o, ln1_g, ln1_b, w_router, b_router, w_e1, w_e3,
           w_e2, w_s1, w_s3, w_s2, ln2_g, ln2_b):
    p = dict(w_in=w_in, conv_w=conv_w, conv_b=conv_b, lru_wa=lru_wa, lru_ba=lru_ba, lru_wx=lru_wx,
             lru_bx=lru_bx, lru_lam=lru_lam, w_pa=w_pa, w_pb=w_pb, w_o=w_o, ln1_g=ln1_g, ln1_b=ln1_b,
             w_router=w_router, b_router=b_router, w_e1=w_e1, w_e3=w_e3, w_e2=w_e2, w_s1=w_s1,
             w_s3=w_s3, w_s2=w_s2, ln2_g=ln2_g, ln2_b=ln2_b)
    caches = (cache_kv_w128, cache_kv_w512, cache_kv_w2048)
    n_p, s_len, d = x_prompt.shape
    n_s, t_new, _ = x_sample.shape
    assert d == D_MODEL and t_new <= SAMPLE_ROWS and t_new >= CONV_W - 1
    dims = (n_p, s_len, n_s, t_new)

    n_c = n_p + n_s
    c_all = jnp.pad(jnp.concatenate([c_prompt, c_sample], axis=0), ((0, -n_c % SUBLANE), (0, 0)))

    xp = x_prompt.reshape(n_p * s_len, d)
    xs = jnp.pad(x_sample, ((0, 0), (0, SAMPLE_ROWS - t_new), (0, 0))).reshape(n_s * SAMPLE_ROWS, d)
    outs = dict(kv_p=[], kv_s=[], h_p=[], h_s=[], conv_p=[], conv_s=[])
    biases = ([_prompt_bias(rel_bias, g, win // dil, dil) for g, (win, dil) in enumerate(ATT_GROUPS)],
              [_sample_bias(rel_bias, g, win, dil, t_new, caches[g].shape[3])
               for g, (win, dil) in enumerate(ATT_GROUPS)])
    for l in range(DEPTH):
        mod = _matmul(c_all, w_mod, l, name="adaln_mod", bias=b_mod, silu_a=True)
        gp = _Group(n_p * s_len, _tile(s_len, TM_TOKEN), mod[:n_p].reshape(n_p, 1, 6 * d), False, s_len)
        gs = _Group(n_s * SAMPLE_ROWS, n_s * SAMPLE_ROWS, jnp.repeat(mod[n_p:n_c], SAMPLE_ROWS, axis=0),
                    True, SAMPLE_ROWS)
        xp, xs, kv_p, kv_s, hl_p, hl_s, conv_p, conv_s = _layer(
            l, xp, xs, gp, gs, p, biases, caches, state_rglru_h, state_conv, dims)
        for key, val in zip(("kv_p", "kv_s", "h_p", "h_s", "conv_p", "conv_s"),
                            (kv_p, kv_s, hl_p, hl_s, conv_p, conv_s)):
            outs[key].append(val)

    def stack_kv(per_layer, g):
        return jnp.stack([r[g] for r in per_layer], axis=1)

    y_prompt = xp.reshape(n_p, s_len, d)
    y_sample = xs.reshape(n_s, SAMPLE_ROWS, d)[:, :t_new]
    return (y_prompt, y_sample,
            stack_kv(outs["kv_p"], 0), stack_kv(outs["kv_p"], 1), stack_kv(outs["kv_p"], 2),
            jnp.stack(outs["h_p"], axis=1), jnp.stack(outs["conv_p"], axis=1),
            stack_kv(outs["kv_s"], 0), stack_kv(outs["kv_s"], 1), stack_kv(outs["kv_s"], 2),
            jnp.stack(outs["h_s"], axis=1), jnp.stack(outs["conv_s"], axis=1))
```

```python
import functools
import math

import numpy as np
import jax
import jax.numpy as jnp
from jax import lax
from jax.experimental import pallas as pl
from jax.experimental.pallas import tpu as pltpu

bf16 = jnp.bfloat16
f32 = jnp.float32

D_MODEL = 4096
DEPTH = 2
LRU_BLOCKS = 16
CONV_W = 4
LRU_C = 8.0
HEAD_DIM = 128
HEADS_PER_GROUP = 8
ATT_GROUPS = ((128, 1), (512, 4), (2048, 16))
REL_BUCKETS = 32
REL_MAX_DIST = 2048
N_EXPERTS = 64
N_EXPERT_GROUPS = 8
TOPK_GROUPS = 4
TOP_K = 8
ROUTED_SCALE = 2.5
LN_EPS = 1e-5

V7X_VMEM_LIMIT_BYTES = 56 * 1024 * 1024
LANE = 128
SUBLANE = 8
SAMPLE_ROWS = 8
TM_MATMUL = 1024
TN_MATMUL = 512
TM_MERGE = 1024
TN_MERGE = 256
TM_TOKEN = 256
T_SCAN = 256
TM_EXPERT = 256
CH_EXPERT = 512
TN_EXPERT = 2048
T_COMBINE = 128
DMA_ISSUE_UNROLL = 16


def _derived():
    d_rnn = D_MODEL
    gw = HEADS_PER_GROUP * HEAD_DIM
    qkv = len(ATT_GROUPS) * gw
    return dict(d_rnn=d_rnn, gw=gw, qkv=qkv, in_cols=d_rnn + 3 * qkv + 2 * D_MODEL,
                dn_alpha=(2 * DEPTH) ** 0.25, att_scale=HEAD_DIM ** -0.5,
                lru_block=d_rnn // LRU_BLOCKS)


def _cparams(*sem):
    return pltpu.CompilerParams(dimension_semantics=sem, vmem_limit_bytes=V7X_VMEM_LIMIT_BYTES)


def _tile(n, pref):
    return pref if n % pref == 0 else n


def _nt_dot(a, b):
    return lax.dot_general(a, b, (((1,), (1,)), ((), ())), preferred_element_type=f32)


def _mm_body(*refs, silu_a, has_bias):
    if has_bias:
        a_ref, b_ref, bias_ref, o_ref = refs
    else:
        a_ref, b_ref, o_ref = refs
    a = a_ref[...]
    if silu_a:
        a = a * jax.nn.sigmoid(a)
    acc = jnp.dot(a.astype(bf16), b_ref[...].astype(bf16), preferred_element_type=f32)
    if has_bias:
        acc = acc + bias_ref[...]
    o_ref[...] = acc.astype(o_ref.dtype)


def _matmul(a, w, l, *, name, bias=None, silu_a=False, out_dtype=f32):
    m, k = a.shape
    n = w.shape[-1]
    tm, tn = _tile(m, TM_MATMUL), _tile(n, TN_MATMUL)
    in_specs = [pl.BlockSpec((tm, k), lambda i, j: (i, 0)),
                pl.BlockSpec((None, k, tn), lambda i, j: (l, 0, j))]
    args = [a, w]
    if bias is not None:
        in_specs.append(pl.BlockSpec((None, 1, tn), lambda i, j: (l, 0, j)))
        args.append(bias.reshape(bias.shape[0], 1, n))
    return pl.pallas_call(
        functools.partial(_mm_body, silu_a=silu_a, has_bias=bias is not None),
        grid=(m // tm, n // tn), in_specs=in_specs,
        out_specs=pl.BlockSpec((tm, tn), lambda i, j: (i, j)),
        out_shape=jax.ShapeDtypeStruct((m, n), out_dtype),
        compiler_params=_cparams("parallel", "arbitrary"), name=name)(*args)


class _Group:
    def __init__(self, rows, tm, mod, per_row, rows_per_seq):
        self.rows, self.tm, self.mod, self.per_row, self.rows_per_seq = rows, tm, mod, per_row, rows_per_seq

    def mod_spec(self, which):
        d = D_MODEL
        if self.per_row:
            return pl.BlockSpec((self.tm, d), lambda i: (i, which))
        tiles = self.rows_per_seq // self.tm
        return pl.BlockSpec((None, 1, d), lambda i: (i // tiles, 0, which))

    def row_spec(self, width):
        return pl.BlockSpec((self.tm, width), lambda i: (i, 0))


def _param_spec(l, width):
    return pl.BlockSpec((None, 1, width), lambda i: (l, 0, 0))


def _modulate_body(x_ref, sc_ref, sh_ref, o_ref):
    o_ref[...] = (x_ref[...] * (1.0 + sc_ref[...]) + sh_ref[...]).astype(o_ref.dtype)


def _modulate(x, grp, *, name):
    d = D_MODEL
    return pl.pallas_call(
        _modulate_body, grid=(grp.rows // grp.tm,),
        in_specs=[grp.row_spec(d), grp.mod_spec(1), grp.mod_spec(0)],
        out_specs=grp.row_spec(d), out_shape=jax.ShapeDtypeStruct((grp.rows, d), bf16),
        compiler_params=_cparams("parallel"), name=name)(x, grp.mod, grp.mod)


def _log_sigmoid(x):
    return -(jnp.maximum(-x, 0.0) + jnp.log1p(jnp.exp(-jnp.abs(x))))


def _neg_expm1(y):
    t = jnp.tanh(0.5 * y)
    return -2.0 * t / (1.0 - t)


def _rglru_body(x_ref, cp_ref, h0_ref, cw_ref, cb_ref, wa_ref, ba_ref, wx_ref, bx_ref, lam_ref,
                hr_ref, hl_ref, prev_scr, h_scr, *, tt, t_last):
    i = pl.program_id(2)

    @pl.when(i == 0)
    def _():
        if tt > SUBLANE:
            prev_scr[0:tt - SUBLANE, :] = jnp.zeros((tt - SUBLANE, prev_scr.shape[1]), f32)
        prev_scr[tt - SUBLANE:tt, :] = cp_ref[...]
        h_scr[...] = h0_ref[...]

    x = x_ref[...]
    prev = prev_scr[...]
    row = lax.broadcasted_iota(jnp.int32, x.shape, 0)

    def shifted(k):
        return jnp.where(row < k, pltpu.roll(prev, k, 0), pltpu.roll(x, k, 0))

    cw = cw_ref[...]
    xc = cb_ref[...] + shifted(3) * cw[0:1]
    xc = xc + shifted(2) * cw[1:2]
    xc = xc + shifted(1) * cw[2:3]
    xc = xc + x * cw[3:4]
    prev_scr[...] = x

    xcb = xc.astype(bf16)
    r = jax.nn.sigmoid(jnp.dot(xcb, wa_ref[...].astype(bf16), preferred_element_type=f32) + ba_ref[...])
    gi = jax.nn.sigmoid(jnp.dot(xcb, wx_ref[...].astype(bf16), preferred_element_type=f32) + bx_ref[...])
    log_a = LRU_C * r * _log_sigmoid(lam_ref[...])
    a = jnp.exp(log_a)
    b = jnp.sqrt(_neg_expm1(2.0 * log_a)) * (gi * xc)

    s = 1
    while s < tt:
        a_sh = jnp.where(row < s, 1.0, pltpu.roll(a, s, 0))
        b_sh = jnp.where(row < s, 0.0, pltpu.roll(b, s, 0))
        b = a * b_sh + b
        a = a * a_sh
        s *= 2
    h = a * h_scr[...] + b
    hr_ref[...] = h.astype(hr_ref.dtype)
    h_scr[...] = h[tt - 1:tt, :]
    hl_ref[...] = h[t_last:t_last + 1, :]


def _rglru(z, conv_prev8, h0, p, l, *, n_seq, t_seq, t_last, name):
    dd = _derived()
    lb = dd["lru_block"]
    tt = _tile(t_seq, T_SCAN)
    n_t = t_seq // tt
    d_rnn = dd["d_rnn"]

    def vec(arr):
        return arr.reshape(DEPTH, 1, d_rnn)

    vspec = pl.BlockSpec((None, 1, lb), lambda b, n, i: (l, 0, n))
    wspec = pl.BlockSpec((None, None, lb, lb), lambda b, n, i: (l, n, 0, 0))
    hr, h_last = pl.pallas_call(
        functools.partial(_rglru_body, tt=tt, t_last=t_last),
        grid=(n_seq, LRU_BLOCKS, n_t),
        in_specs=[pl.BlockSpec((tt, lb), lambda b, n, i: (b * n_t + i, n)),
                  pl.BlockSpec((None, SUBLANE, lb), lambda b, n, i: (b, 0, n)),
                  pl.BlockSpec((None, 1, lb), lambda b, n, i: (b, 0, n)),
                  pl.BlockSpec((None, CONV_W, lb), lambda b, n, i: (l, 0, n)),
                  vspec, wspec, vspec, wspec, vspec, vspec],
        out_specs=[pl.BlockSpec((tt, lb), lambda b, n, i: (b * n_t + i, n)),
                   pl.BlockSpec((None, 1, lb), lambda b, n, i: (b, 0, n))],
        out_shape=[jax.ShapeDtypeStruct((n_seq * t_seq, d_rnn), bf16),
                   jax.ShapeDtypeStruct((n_seq, 1, d_rnn), f32)],
        scratch_shapes=[pltpu.VMEM((tt, lb), f32), pltpu.VMEM((1, lb), f32)],
        compiler_params=_cparams("parallel", "parallel", "arbitrary"), name=name,
    )(z, conv_prev8, h0.reshape(n_seq, 1, d_rnn), p["conv_w"], vec(p["conv_b"]), p["lru_wa"],
      vec(p["lru_ba"]), p["lru_wx"], vec(p["lru_bx"]), vec(p["lru_lam"]))
    return hr, h_last.reshape(n_seq, d_rnn)


def _rel_bucket(dist):
    dist = np.asarray(dist, np.int32)
    max_exact = REL_BUCKETS // 2
    df = np.maximum(dist, 1).astype(np.float32)
    large = max_exact + (np.log(df / np.float32(max_exact)) / np.float32(math.log(REL_MAX_DIST / max_exact))
                         * np.float32(REL_BUCKETS - max_exact)).astype(np.int32)
    large = np.minimum(large, REL_BUCKETS - 1)
    return np.where(dist < max_exact, dist, large)


def _bias_lookup(tab, bucket, ok):
    flat = jnp.asarray(bucket.reshape(-1), jnp.int32)
    onehot = (flat[:, None] == jnp.arange(REL_BUCKETS, dtype=jnp.int32)[None, :]).astype(f32)
    vals = jnp.dot(onehot, tab.astype(f32), precision=lax.Precision.HIGHEST)
    vals = vals.reshape(bucket.shape + (tab.shape[1],))
    return jnp.where(jnp.asarray(ok)[..., None], vals, -jnp.inf)


def _prompt_bias(rel_bias, g, band, dil):
    rel = np.arange(band)[:, None] + band - np.arange(2 * band)[None, :]
    in_band = (rel >= 0) & (rel <= band)
    tab = rel_bias[:, g * HEADS_PER_GROUP:(g + 1) * HEADS_PER_GROUP]
    vals = _bias_lookup(tab, _rel_bucket(np.clip(rel, 0, band) * dil), in_band)
    return vals.transpose(2, 0, 1)


def _sample_bias(rel_bias, g, win, dil, t_new, lb_cache):
    nk = win // dil + 1
    t = np.arange(SAMPLE_ROWS)[:, None]
    tab = rel_bias[:, g * HEADS_PER_GROUP:(g + 1) * HEADS_PER_GROUP]

    def table(delta):
        ok = (delta >= 0) & (delta % dil == 0) & (delta // dil < nk)
        vals = _bias_lookup(tab, _rel_bucket(np.maximum(delta, 0)), ok)
        vals = jnp.where(jnp.asarray(t < t_new)[:, :, None], vals, 0.0)
        return vals.transpose(2, 0, 1)

    cache = table(lb_cache + t - np.arange(lb_cache)[None, :])
    j = np.arange(SAMPLE_ROWS)[None, :]
    new = table(np.where(j < t_new, t - j, -1))
    return cache, new


def _attn_prompt_body(*refs, s_len, scale):
    n_g = len(ATT_GROUPS)
    o_ref, o_scr, l_scr = refs[4 * n_g:]
    for g, (win, dil) in enumerate(ATT_GROUPS):
        q_ref, k_ref, v_ref, b_ref = refs[4 * g:4 * g + 4]
        band = win // dil
        nb = (s_len // dil) // band
        bias = b_ref[...]
        for r in range(dil):
            for blk in range(nb):
                def rows(first_blk, n_rows):
                    start = r + first_blk * band * dil
                    return pl.ds(start, n_rows) if dil == 1 else pl.ds(start, n_rows, stride=dil)
                qrows = rows(blk, band)
                krows, bb = (qrows, bias[:, band:]) if blk == 0 else (rows(blk - 1, 2 * band), bias)
                q = q_ref[qrows, :].astype(bf16)
                k = k_ref[krows, :].astype(bf16)
                v = v_ref[krows, :].astype(bf16)
                s = _nt_dot(q, k) * scale + bb
                mx = jnp.max(s, axis=-1, keepdims=True)
                p = jnp.exp(s - mx)
                den = jnp.sum(p, axis=-1, keepdims=True)
                o = jnp.dot((p / den).astype(bf16), v, preferred_element_type=f32)
                o_scr[g, qrows, :] = o
                l_scr[g, qrows, :] = jnp.broadcast_to(mx + jnp.log(den), o.shape)
    ch = 256
    for c in range(s_len // ch):
        sl = pl.ds(c * ch, ch)
        ls = [l_scr[g, sl, :] for g in range(n_g)]
        m = functools.reduce(jnp.maximum, ls)
        es = [jnp.exp(x - m) for x in ls]
        tot = functools.reduce(lambda a, b: a + b, es)
        out = functools.reduce(lambda a, b: a + b, [(es[g] / tot) * o_scr[g, sl, :] for g in range(n_g)])
        o_ref[sl, :] = out.astype(o_ref.dtype)


def _attn_prompt(z, biases, *, n_seq, s_len, name):
    dd = _derived()
    gw, d_rnn, qkv = dd["gw"], dd["d_rnn"], dd["qkv"]
    in_specs, args = [], []
    for g, (win, dil) in enumerate(ATT_GROUPS):
        band = win // dil
        assert (s_len // dil) % band == 0
        for part in range(3):
            c0 = (d_rnn + part * qkv + g * gw) // HEAD_DIM
            in_specs.append(pl.BlockSpec((s_len, HEAD_DIM), lambda b, h, c0=c0: (b, c0 + h)))
            args.append(z)
        in_specs.append(pl.BlockSpec((None, band, 2 * band), lambda b, h: (h, 0, 0)))
        args.append(biases[g])
    n_g = len(ATT_GROUPS)
    return pl.pallas_call(
        functools.partial(_attn_prompt_body, s_len=s_len, scale=dd["att_scale"]),
        grid=(n_seq, HEADS_PER_GROUP), in_specs=in_specs,
        out_specs=pl.BlockSpec((s_len, HEAD_DIM), lambda b, h: (b, h)),
        out_shape=jax.ShapeDtypeStruct((n_seq * s_len, gw), bf16),
        scratch_shapes=[pltpu.VMEM((n_g, s_len, HEAD_DIM), f32), pltpu.VMEM((n_g, s_len, HEAD_DIM), f32)],
        compiler_params=_cparams("parallel", "parallel"), name=name)(*args)


def _attn_sample_body(*refs, scale):
    n_g = len(ATT_GROUPS)
    o_ref = refs[7 * n_g]
    outs, lses = [], []
    for g in range(n_g):
        q_ref, kn_ref, vn_ref, kc_ref, vc_ref, bc_ref, bn_ref = refs[7 * g:7 * g + 7]
        q = q_ref[...].astype(bf16)
        s_c = _nt_dot(q, kc_ref[...].astype(bf16)) * scale + bc_ref[...]
        s_n = _nt_dot(q, kn_ref[...].astype(bf16)) * scale + bn_ref[...]
        mx = jnp.maximum(jnp.max(s_c, axis=-1, keepdims=True), jnp.max(s_n, axis=-1, keepdims=True))
        p_c = jnp.exp(s_c - mx)
        p_n = jnp.exp(s_n - mx)
        den = jnp.sum(p_c, axis=-1, keepdims=True) + jnp.sum(p_n, axis=-1, keepdims=True)
        o = jnp.dot((p_c / den).astype(bf16), vc_ref[...].astype(bf16), preferred_element_type=f32)
        o = o + jnp.dot((p_n / den).astype(bf16), vn_ref[...].astype(bf16), preferred_element_type=f32)
        outs.append(o)
        lses.append(mx + jnp.log(den))
    m = functools.reduce(jnp.maximum, lses)
    es = [jnp.exp(x - m) for x in lses]
    tot = functools.reduce(lambda a, b: a + b, es)
    o_ref[...] = functools.reduce(lambda a, b: a + b, [(es[g] / tot) * outs[g] for g in range(n_g)])


def _attn_sample(z3, caches, biases, l, *, n_seq, t_new, name):
    dd = _derived()
    gw, d_rnn, qkv = dd["gw"], dd["d_rnn"], dd["qkv"]
    in_specs, args = [], []
    for g, (win, dil) in enumerate(ATT_GROUPS):
        cache = caches[g]
        lb_cache = cache.shape[3]
        cache = cache.reshape(n_seq, DEPTH, 2, lb_cache, gw)
        for part in range(3):
            c0 = (d_rnn + part * qkv + g * gw) // HEAD_DIM
            in_specs.append(pl.BlockSpec((None, SAMPLE_ROWS, HEAD_DIM), lambda b, h, c0=c0: (b, 0, c0 + h)))
            args.append(z3)
        for kv in range(2):
            in_specs.append(pl.BlockSpec((None, None, None, lb_cache, HEAD_DIM),
                                         lambda b, h, kv=kv: (b, l, kv, 0, h)))
            args.append(cache)
        b_cache, b_new = biases[g]
        in_specs.append(pl.BlockSpec((None, SAMPLE_ROWS, lb_cache), lambda b, h: (h, 0, 0)))
        in_specs.append(pl.BlockSpec((None, SAMPLE_ROWS, SAMPLE_ROWS), lambda b, h: (h, 0, 0)))
        args += [b_cache, b_new]
    out = pl.pallas_call(
        functools.partial(_attn_sample_body, scale=dd["att_scale"]),
        grid=(n_seq, HEADS_PER_GROUP), in_specs=in_specs,
        out_specs=pl.BlockSpec((None, SAMPLE_ROWS, HEAD_DIM), lambda b, h: (b, 0, h)),
        out_shape=jax.ShapeDtypeStruct((n_seq, SAMPLE_ROWS, gw), f32),
        compiler_params=_cparams("parallel", "parallel"), name=name)(*args)
    return out.reshape(n_seq * SAMPLE_ROWS, gw)


def _merge_body(hr_ref, att_ref, wa_ref, wb_ref, ga_ref, gb_ref, o_ref):
    pa = jnp.dot(hr_ref[...].astype(bf16), wa_ref[...].astype(bf16), preferred_element_type=f32)
    pb = jnp.dot(att_ref[...].astype(bf16), wb_ref[...].astype(bf16), preferred_element_type=f32)
    o_ref[...] = (jax.nn.sigmoid(ga_ref[...]) * pa + jax.nn.sigmoid(gb_ref[...]) * pb).astype(o_ref.dtype)


def _merge(hr, att, z, p, l, *, name):
    dd = _derived()
    m = hr.shape[0]
    d = D_MODEL
    tm, tn = _tile(m, TM_MERGE), _tile(d, TN_MERGE)
    ga0 = (dd["d_rnn"] + 3 * dd["qkv"]) // tn
    gb0 = ga0 + d // tn
    return pl.pallas_call(
        _merge_body, grid=(m // tm, d // tn),
        in_specs=[pl.BlockSpec((tm, hr.shape[1]), lambda i, j: (i, 0)),
                  pl.BlockSpec((tm, att.shape[1]), lambda i, j: (i, 0)),
                  pl.BlockSpec((None, hr.shape[1], tn), lambda i, j: (l, 0, j)),
                  pl.BlockSpec((None, att.shape[1], tn), lambda i, j: (l, 0, j)),
                  pl.BlockSpec((tm, tn), lambda i, j: (i, ga0 + j)),
                  pl.BlockSpec((tm, tn), lambda i, j: (i, gb0 + j))],
        out_specs=pl.BlockSpec((tm, tn), lambda i, j: (i, j)),
        out_shape=jax.ShapeDtypeStruct((m, d), bf16),
        compiler_params=_cparams("parallel", "arbitrary"), name=name)(hr, att, p["w_pa"], p["w_pb"], z, z)


def _layer_norm(v, g, b):
    mu = jnp.mean(v, axis=-1, keepdims=True)
    var = jnp.mean(jnp.square(v - mu), axis=-1, keepdims=True)
    return (v - mu) * lax.rsqrt(var + LN_EPS) * g + b


def _pack_halves(h):
    half = h.shape[1] // 2
    lo = lax.bitcast_convert_type(h[:, :half].astype(bf16).astype(f32), jnp.uint32)
    hi = lax.bitcast_convert_type(h[:, half:].astype(bf16).astype(f32), jnp.uint32)
    return (lo >> 16) | (hi & jnp.uint32(0xFFFF0000))


def _unpack_halves(xu):
    lo = lax.bitcast_convert_type(xu << 16, f32).astype(bf16)
    hi = lax.bitcast_convert_type(xu & jnp.uint32(0xFFFF0000), f32).astype(bf16)
    return lo, hi


def _split_bf16(x):
    hi = x.astype(bf16)
    return hi, (x - hi.astype(f32)).astype(bf16)


def _first_max(vals, idx):
    m = functools.reduce(jnp.maximum, [jnp.max(v, axis=0, keepdims=True) for v in vals])
    big = float(N_EXPERTS)
    cand = [jnp.min(jnp.where(v == m, i, big), axis=0, keepdims=True) for v, i in zip(vals, idx)]
    return m, functools.reduce(jnp.minimum, cand)


def _ln_router_body(x_ref, mix_ref, g1_ref, sc_ref, sh_ref, lg_ref, lb_ref, wr_ref, br_ref, base_ref,
                    x1_ref, hp_ref, topi_ref, topw_ref, rank_ref, cnt_ref, *, alpha):
    i = pl.program_id(0)
    x1 = _layer_norm(alpha * x_ref[...] + g1_ref[...] * mix_ref[...], lg_ref[...], lb_ref[...])
    x1_ref[...] = x1
    h = x1 * (1.0 + sc_ref[...]) + sh_ref[...]
    hp_ref[...] = _pack_halves(h)

    h_hi, h_lo = _split_bf16(h)
    w_hi, w_lo = _split_bf16(wr_ref[...])
    logits = _nt_dot(w_hi, h_hi) + _nt_dot(w_hi, h_lo) + _nt_dot(w_lo, h_hi)
    s = jax.nn.sigmoid(logits)
    sel = s + br_ref[...]
    tm = s.shape[1]
    per = N_EXPERTS // N_EXPERT_GROUPS
    sub = lax.broadcasted_iota(jnp.int32, (per, tm), 0).astype(f32)
    sel_g = [sel[g * per:(g + 1) * per, :] for g in range(N_EXPERT_GROUPS)]
    s_g = [s[g * per:(g + 1) * per, :] for g in range(N_EXPERT_GROUPS)]
    idx_g = [sub + float(g * per) for g in range(N_EXPERT_GROUPS)]
    neg = -jnp.inf

    grp = []
    for g in range(N_EXPERT_GROUPS):
        m1, i1 = _first_max([sel_g[g]], [idx_g[g]])
        m2 = jnp.max(jnp.where(idx_g[g] == i1, neg, sel_g[g]), axis=0, keepdims=True)
        grp.append(m1 + m2)
    keep = [jnp.zeros((1, tm), jnp.bool_) for _ in range(N_EXPERT_GROUPS)]
    for _ in range(TOPK_GROUPS):
        m = functools.reduce(jnp.maximum, grp)
        found = jnp.zeros((1, tm), jnp.bool_)
        for g in range(N_EXPERT_GROUPS):
            hit = (grp[g] == m) & jnp.logical_not(found)
            found = found | hit
            keep[g] = keep[g] | hit
            grp[g] = jnp.where(hit, neg, grp[g])
    cand = [jnp.where(keep[g], sel_g[g], neg) for g in range(N_EXPERT_GROUPS)]

    @pl.when(i == 0)
    def _():
        cnt_ref[...] = base_ref[...]

    picked = [jnp.zeros((per, tm), f32) for _ in range(N_EXPERT_GROUPS)]
    top_i, top_w = [], []
    for _ in range(TOP_K):
        _, ik = _first_max(cand, idx_g)
        hits = [idx_g[g] == ik for g in range(N_EXPERT_GROUPS)]
        wk = functools.reduce(lambda a, b: a + b,
                              [jnp.sum(jnp.where(hits[g], s_g[g], 0.0), axis=0, keepdims=True)
                               for g in range(N_EXPERT_GROUPS)])
        for g in range(N_EXPERT_GROUPS):
            cand[g] = jnp.where(hits[g], neg, cand[g])
            picked[g] = jnp.where(hits[g], 1.0, picked[g])
        top_i.append(ik)
        top_w.append(wk)
    wsum = functools.reduce(lambda a, b: a + b, top_w)

    onehot = jnp.concatenate(picked, axis=0)
    r_i = lax.broadcasted_iota(jnp.int32, (tm, tm), 0)
    c_i = lax.broadcasted_iota(jnp.int32, (tm, tm), 1)
    before = (r_i < c_i).astype(bf16)
    prefix = jnp.dot(onehot.astype(bf16), before, preferred_element_type=f32) + cnt_ref[:, 0:1]
    cnt_ref[...] = cnt_ref[...] + jnp.sum(onehot, axis=1, keepdims=True)
    pre_g = [prefix[g * per:(g + 1) * per, :] for g in range(N_EXPERT_GROUPS)]
    for k in range(TOP_K):
        rk = functools.reduce(lambda a, b: a + b,
                              [jnp.sum(jnp.where(idx_g[g] == top_i[k], pre_g[g], 0.0), axis=0, keepdims=True)
                               for g in range(N_EXPERT_GROUPS)])
        topi_ref[k:k + 1, :] = top_i[k].astype(jnp.int32)
        topw_ref[k:k + 1, :] = top_w[k] / wsum * ROUTED_SCALE
        rank_ref[k:k + 1, :] = rk.astype(jnp.int32)


def _ln_router(x, mix, grp, p, l, base_counts, *, name):
    dd = _derived()
    d = D_MODEL
    rows, tm = grp.rows, grp.tm
    w_rt = jnp.swapaxes(p["w_router"], 1, 2)
    b_r = p["b_router"].reshape(DEPTH, N_EXPERTS, 1)
    tok = lambda dt: jax.ShapeDtypeStruct((TOP_K, rows), dt)
    tok_spec = pl.BlockSpec((TOP_K, tm), lambda i: (0, i))
    return pl.pallas_call(
        functools.partial(_ln_router_body, alpha=dd["dn_alpha"]),
        grid=(rows // tm,),
        in_specs=[grp.row_spec(d), grp.row_spec(d), grp.mod_spec(2), grp.mod_spec(4), grp.mod_spec(3),
                  _param_spec(l, d), _param_spec(l, d),
                  pl.BlockSpec((None, N_EXPERTS, d), lambda i: (l, 0, 0)),
                  pl.BlockSpec((None, N_EXPERTS, 1), lambda i: (l, 0, 0)),
                  pl.BlockSpec((N_EXPERTS, LANE), lambda i: (0, 0))],
        out_specs=[grp.row_spec(d), grp.row_spec(d // 2), tok_spec, tok_spec, tok_spec,
                   pl.BlockSpec((N_EXPERTS, LANE), lambda i: (0, 0))],
        out_shape=[jax.ShapeDtypeStruct((rows, d), f32), jax.ShapeDtypeStruct((rows, d // 2), jnp.uint32),
                   tok(jnp.int32), tok(f32), tok(jnp.int32),
                   jax.ShapeDtypeStruct((N_EXPERTS, LANE), f32)],
        compiler_params=_cparams("arbitrary"), name=name,
    )(x, mix, grp.mod, grp.mod, grp.mod, p["ln1_g"].reshape(DEPTH, 1, d), p["ln1_b"].reshape(DEPTH, 1, d),
      w_rt, b_r, base_counts)


def _gather_body(idx_ref, nv_ref, src_ref, o_ref, sem, *, tg):
    i = pl.program_id(0)
    base = i * tg
    n_valid = nv_ref[i]
    group = DMA_ISSUE_UNROLL

    o_ref[...] = jnp.zeros(o_ref.shape, o_ref.dtype)

    def row_copy(r, src_row):
        return pltpu.make_async_copy(src_ref.at[pl.ds(src_row, 1), :], o_ref.at[pl.ds(r, 1), :], sem)

    def start_group(j, carry):
        for q in range(group):
            r = j * group + q
            row_copy(r, idx_ref[base + r]).start(priority=q % 2)
        return carry

    def start_one(r, carry):
        row_copy(r, idx_ref[base + r]).start()
        return carry

    n_groups = lax.shift_right_logical(n_valid, group.bit_length() - 1)
    lax.fori_loop(0, n_groups, start_group, 0)
    lax.fori_loop(n_groups * group, n_valid, start_one, 0)
    for bit in range(tg.bit_length()):
        n = 1 << bit
        if n <= tg:
            @pl.when((n_valid & n) != 0)
            def _(n=n):
                pltpu.make_async_copy(src_ref.at[pl.ds(0, n), :], o_ref.at[pl.ds(0, n), :], sem).wait()


def _gather_rows(src, idx, n_valid, tg, *, name):
    r_out, width = idx.shape[0], src.shape[1]
    assert r_out % tg == 0 and tg & (tg - 1) == 0 and tg % DMA_ISSUE_UNROLL == 0
    return pl.pallas_call(
        functools.partial(_gather_body, tg=tg),
        grid_spec=pltpu.PrefetchScalarGridSpec(
            num_scalar_prefetch=2, grid=(r_out // tg,),
            in_specs=[pl.BlockSpec(memory_space=pl.ANY)],
            out_specs=pl.BlockSpec((tg, width), lambda i, idx_ref, nv_ref: (i, 0)),
            scratch_shapes=[pltpu.SemaphoreType.DMA(())]),
        out_shape=jax.ShapeDtypeStruct((r_out, width), src.dtype),
        compiler_params=_cparams("arbitrary"), name=name)(idx, n_valid, src)


def _expert_changed(be_ref, i):
    return (i == 0) | (be_ref[i] != be_ref[jnp.maximum(i - 1, 0)])


def _ffn_a_body(be_ref, nu_ref, x_ref, w1_ref, w3_ref, o_ref, w1_scr, w3_scr):
    i = pl.program_id(1)

    @pl.when(i < nu_ref[0])
    def _():
        @pl.when(_expert_changed(be_ref, i))
        def _():
            w1_scr[...] = w1_ref[...].astype(bf16)
            w3_scr[...] = w3_ref[...].astype(bf16)

        lo, hi = _unpack_halves(x_ref[...])
        half = lo.shape[1]
        h1 = (jnp.dot(lo, w1_scr[0:half, :], preferred_element_type=f32)
              + jnp.dot(hi, w1_scr[half:, :], preferred_element_type=f32))
        h3 = (jnp.dot(lo, w3_scr[0:half, :], preferred_element_type=f32)
              + jnp.dot(hi, w3_scr[half:, :], preferred_element_type=f32))
        o_ref[...] = (h1 * jax.nn.sigmoid(h1) * h3).astype(o_ref.dtype)

    @pl.when(i >= nu_ref[0])
    def _():
        o_ref[...] = jnp.zeros(o_ref.shape, o_ref.dtype)


def _ffn_b_body(be_ref, nu_ref, a_ref, w2_ref, o_ref, w2_scr):
    i = pl.program_id(1)

    @pl.when(i < nu_ref[0])
    def _():
        @pl.when(_expert_changed(be_ref, i))
        def _():
            w2_scr[...] = w2_ref[...].astype(bf16)

        o_ref[...] = jnp.dot(a_ref[...], w2_scr[...], preferred_element_type=f32)

    @pl.when(i >= nu_ref[0])
    def _():
        o_ref[...] = jnp.zeros(o_ref.shape, o_ref.dtype)


def _expert_ffn(xp, w1, w3, w2, lead, block_e, n_used, *, tm, name):
    r_rows, half = xp.shape
    d = 2 * half
    d_hid = w1.shape[-1]
    n_blocks = r_rows // tm
    ch, tn = _tile(d_hid, CH_EXPERT), _tile(d, TN_EXPERT)
    nl = len(lead)

    def blk(i, nu_ref):
        return jnp.minimum(i, nu_ref[0] - 1)

    def w_spec(shape, col_axis):
        def index(c, i, be_ref, nu_ref):
            e = be_ref[blk(i, nu_ref)]
            return lead + ((e, 0, c) if col_axis == 2 else (e, c, 0))
        return pl.BlockSpec((None,) * (nl + 1) + shape, index)

    act = pl.pallas_call(
        _ffn_a_body,
        grid_spec=pltpu.PrefetchScalarGridSpec(
            num_scalar_prefetch=2, grid=(d_hid // ch, n_blocks),
            in_specs=[pl.BlockSpec((tm, half), lambda c, i, be, nu: (blk(i, nu), 0)),
                      w_spec((d, ch), 2), w_spec((d, ch), 2)],
            out_specs=pl.BlockSpec((tm, ch), lambda c, i, be, nu: (i, c)),
            scratch_shapes=[pltpu.VMEM((d, ch), bf16), pltpu.VMEM((d, ch), bf16)]),
        out_shape=jax.ShapeDtypeStruct((r_rows, d_hid), bf16),
        compiler_params=_cparams("arbitrary", "arbitrary"), name=name + "_up")(block_e, n_used, xp, w1, w3)
    return pl.pallas_call(
        _ffn_b_body,
        grid_spec=pltpu.PrefetchScalarGridSpec(
            num_scalar_prefetch=2, grid=(d // tn, n_blocks),
            in_specs=[pl.BlockSpec((tm, d_hid), lambda c, i, be, nu: (blk(i, nu), 0)),
                      w_spec((d_hid, tn), 2)],
            out_specs=pl.BlockSpec((tm, tn), lambda c, i, be, nu: (i, c)),
            scratch_shapes=[pltpu.VMEM((d_hid, tn), bf16)]),
        out_shape=jax.ShapeDtypeStruct((r_rows, d), f32),
        compiler_params=_cparams("arbitrary", "arbitrary"), name=name + "_down")(block_e, n_used, act, w2)


def _combine_body(pos_ref, y_ref, x1_ref, sh_ref, w_ref, g2_ref, lg_ref, lb_ref, o_ref, buf, sem, *, tc, alpha):
    base = pl.program_id(0) * (TOP_K * tc)

    def row_copy(k, t, src_row):
        return pltpu.make_async_copy(y_ref.at[pl.ds(src_row, 1), :], buf.at[k, pl.ds(t, 1), :], sem)

    for k in range(TOP_K):
        def start(t, carry, k=k):
            row_copy(k, t, pos_ref[base + k * tc + t]).start(priority=k % 2)
            return carry
        lax.fori_loop(0, tc, start, 0, unroll=DMA_ISSUE_UNROLL)
    for k in range(TOP_K):
        pltpu.make_async_copy(y_ref.at[pl.ds(0, tc), :], buf.at[k], sem).wait()

    w = w_ref[...]
    ffn = buf[0] * w[:, 0:1]
    for k in range(1, TOP_K):
        ffn = ffn + buf[k] * w[:, k:k + 1]
    ffn = ffn + sh_ref[...]
    o_ref[...] = _layer_norm(alpha * x1_ref[...] + g2_ref[...] * ffn, lg_ref[...], lb_ref[...])


def _combine(y_sorted, pos, topw, x1, shared, grp, p, l, *, name):
    dd = _derived()
    d = D_MODEL
    rows = grp.rows
    tc = _tile(rows, T_COMBINE)
    n_t = rows // tc
    pos_flat = pos.reshape(TOP_K, n_t, tc).transpose(1, 0, 2).reshape(-1)
    tiles = grp.rows_per_seq // tc if not grp.per_row else None

    def mod_spec(which):
        if grp.per_row:
            return pl.BlockSpec((tc, d), lambda i, pr: (i, which))
        return pl.BlockSpec((None, 1, d), lambda i, pr: (i // tiles, 0, which))

    row = pl.BlockSpec((tc, d), lambda i, pr: (i, 0))
    par = pl.BlockSpec((None, 1, d), lambda i, pr: (l, 0, 0))
    return pl.pallas_call(
        functools.partial(_combine_body, tc=tc, alpha=dd["dn_alpha"]),
        grid_spec=pltpu.PrefetchScalarGridSpec(
            num_scalar_prefetch=1, grid=(n_t,),
            in_specs=[pl.BlockSpec(memory_space=pl.ANY), row, row,
                      pl.BlockSpec((tc, TOP_K), lambda i, pr: (i, 0)),
                      mod_spec(5), par, par],
            out_specs=row,
            scratch_shapes=[pltpu.VMEM((TOP_K, tc, d), f32), pltpu.SemaphoreType.DMA(())]),
        out_shape=jax.ShapeDtypeStruct((rows, d), f32),
        compiler_params=_cparams("arbitrary"), name=name,
    )(pos_flat, y_sorted, x1, shared, topw.T, grp.mod, p["ln2_g"].reshape(DEPTH, 1, d),
      p["ln2_b"].reshape(DEPTH, 1, d))


def _routing_tables(topi, rank, counts, tm):
    n_tok = topi.shape[1]
    m = n_tok * TOP_K
    n_blocks = -(-m // tm) + N_EXPERTS
    padded = (counts + tm - 1) // tm * tm
    ends = jnp.cumsum(padded)
    starts = ends - padded
    experts = jnp.arange(N_EXPERTS, dtype=jnp.int32)
    start_of = jnp.sum(jnp.where(topi[:, :, None] == experts, starts, 0), axis=-1)
    pos = start_of + rank
    tok = jnp.broadcast_to(jnp.arange(n_tok, dtype=jnp.int32)[None, :], pos.shape)
    slot_tok = jnp.zeros((n_blocks * tm,), jnp.int32).at[pos.reshape(-1)].set(tok.reshape(-1))
    block_e = jnp.sum(ends[None, :] <= (jnp.arange(n_blocks, dtype=jnp.int32) * tm)[:, None], axis=1)
    block_e = jnp.minimum(block_e, N_EXPERTS - 1).astype(jnp.int32)
    n_used = (ends[-1] // tm).astype(jnp.int32).reshape(1)
    block_start = jnp.arange(n_blocks, dtype=jnp.int32) * tm
    n_valid = jnp.clip((starts + counts)[block_e] - block_start, 0, tm).astype(jnp.int32)
    return pos, slot_tok, block_e, n_used, n_valid


def _token_mixer(x, grp, p, l, biases, *, tag, prompt, n_seq, t_seq, conv_prev8, h0, caches, t_new):
    h = _modulate(x, grp, name=f"modulate_{tag}")
    z = _matmul(h, p["w_in"], l, name=f"in_proj_{tag}")
    hr, h_last = _rglru(z, conv_prev8, h0, p, l, n_seq=n_seq, t_seq=t_seq,
                        t_last=(t_seq - 1) % _tile(t_seq, T_SCAN) if prompt else t_new - 1,
                        name=f"rglru_{tag}")
    if prompt:
        att = _attn_prompt(z, biases, n_seq=n_seq, s_len=t_seq, name=f"attn_{tag}")
    else:
        att = _attn_sample(z.reshape(n_seq, SAMPLE_ROWS, z.shape[1]), caches, biases, l,
                           n_seq=n_seq, t_new=t_new, name=f"attn_{tag}")
    merged = _merge(hr, att, z, p, l, name=f"merge_{tag}")
    mix = _matmul(merged, p["w_o"], l, name=f"out_proj_{tag}")
    return mix, z, h_last


def _layer(l, xp, xs, gp, gs, p, biases, caches, state_h, state_conv, dims):
    dd = _derived()
    n_p, s_len, n_s, t_new = dims
    d = D_MODEL
    d_rnn, gw, qkv = dd["d_rnn"], dd["gw"], dd["qkv"]

    zeros_prev = jnp.zeros((n_p, SUBLANE, d_rnn), f32)
    mix_p, z_p, hl_p = _token_mixer(xp, gp, p, l, biases[0], tag="p", prompt=True, n_seq=n_p, t_seq=s_len,
                                    conv_prev8=zeros_prev, h0=jnp.zeros((n_p, d_rnn), f32),
                                    caches=None, t_new=None)
    prev_s = jnp.pad(state_conv[:, l], ((0, 0), (SUBLANE - (CONV_W - 1), 0), (0, 0)))
    mix_s, z_s, hl_s = _token_mixer(xs, gs, p, l, biases[1], tag="s", prompt=False, n_seq=n_s,
                                    t_seq=SAMPLE_ROWS, conv_prev8=prev_s, h0=state_h[:, l],
                                    caches=caches, t_new=t_new)

    x1_p, hp_p, ti_p, tw_p, rk_p, cnt_p = _ln_router(xp, mix_p, gp, p, l, jnp.zeros((N_EXPERTS, LANE), f32),
                                                     name="ln_router_p")
    x1_s, hp_s, ti_s, tw_s, rk_s, cnt = _ln_router(xs, mix_s, gs, p, l, cnt_p, name="ln_router_s")
    topi = jnp.concatenate([ti_p, ti_s], axis=1)
    topw = jnp.concatenate([tw_p, tw_s], axis=1)
    rank = jnp.concatenate([rk_p, rk_s], axis=1)
    counts = cnt[:, 0].astype(jnp.int32)
    pos, slot_tok, block_e, n_used, n_valid = _routing_tables(topi, rank, counts, TM_EXPERT)

    hp_all = jnp.concatenate([hp_p, hp_s], axis=0)
    x_sorted = _gather_rows(hp_all, slot_tok, n_valid, TM_EXPERT, name="moe_gather")
    y_sorted = _expert_ffn(x_sorted, p["w_e1"], p["w_e3"], p["w_e2"], (l,), block_e, n_used,
                           tm=TM_EXPERT, name="moe_experts")

    def shared(hp, tag):
        rows = hp.shape[0]
        tm = _tile(rows, TM_EXPERT)
        nb = rows // tm
        return _expert_ffn(hp, p["w_s1"], p["w_s3"], p["w_s2"], (), jnp.full((nb,), l, jnp.int32),
                           jnp.full((1,), nb, jnp.int32), tm=tm, name=f"shared_{tag}")

    n_tok_p = gp.rows
    x2_p = _combine(y_sorted, pos[:, :n_tok_p], topw[:, :n_tok_p], x1_p, shared(hp_p, "p"), gp, p, l,
                    name="combine_p")
    x2_s = _combine(y_sorted, pos[:, n_tok_p:], topw[:, n_tok_p:], x1_s, shared(hp_s, "s"), gs, p, l,
                    name="combine_s")

    z_p3 = z_p.reshape(n_p, s_len, -1)
    z_s3 = z_s.reshape(n_s, SAMPLE_ROWS, -1)
    kv_p, kv_s = [], []
    for g, (win, _) in enumerate(ATT_GROUPS):
        keep = min(win, s_len)
        k0 = d_rnn + qkv + g * gw
        v0 = d_rnn + 2 * qkv + g * gw

        def heads(zz, c0, rows):
            return zz[:, rows, c0:c0 + gw].reshape(zz.shape[0], -1, HEADS_PER_GROUP, HEAD_DIM)

        rp = slice(s_len - keep, s_len)
        kv_p.append(jnp.stack([heads(z_p3, k0, rp), heads(z_p3, v0, rp)], axis=1))
        rs = slice(0, t_new)
        kv_s.append(jnp.stack([heads(z_s3, k0, rs), heads(z_s3, v0, rs)], axis=1))
    conv_p = z_p3[:, s_len - (CONV_W - 1):, :d_rnn]
    conv_s = z_s3[:, t_new - (CONV_W - 1):t_new, :d_rnn] if t_new >= CONV_W - 1 else None
    return x2_p, x2_s, kv_p, kv_s, hl_p, hl_s, conv_p, conv_s


def kernel(x_prompt, x_sample, c_prompt, c_sample, cache_kv_w128, cache_kv_w512, cache_kv_w2048,
           state_rglru_h, state_conv, rel_bias, w_mod, b_mod, w_in, conv_w, conv_b, lru_wa, lru_ba,
           lru_wx, lru_bx, lru_lam, w_pa, w_pb, w_o, ln1_g, ln1_b, w_router, b_router, w_e1, w_e3,
           w_e2, w_s1, w_s3, w_s2, ln2_g, ln2_b):
    p = dict(w_in=w_in, conv_w=conv_w, conv_b=conv_b, lru_wa=lru_wa, lru_ba=lru_ba, lru_wx=lru_wx,
             lru_bx=lru_bx, lru_lam=lru_lam, w_pa=w_pa, w_pb=w_pb, w_o=w_o, ln1_g=ln1_g, ln1_b=ln1_b,
             w_router=w_router, b_router=b_router, w_e1=w_e1, w_e3=w_e3, w_e2=w_e2, w_s1=w_s1,
             w_s3=w_s3, w_s2=w_s2, ln2_g=ln2_g, ln2_b=ln2_b)
    caches = (cache_kv_w128, cache_kv_w512, cache_kv_w2048)
    n_p, s_len, d = x_prompt.shape
    n_s, t_new, _ = x_sample.shape
    assert d == D_MODEL and t_new <= SAMPLE_ROWS and t_new >= CONV_W - 1
    dims = (n_p, s_len, n_s, t_new)

    n_c = n_p + n_s
    c_all = jnp.pad(jnp.concatenate([c_prompt, c_sample], axis=0), ((0, -n_c % SUBLANE), (0, 0)))

    xp = x_prompt.reshape(n_p * s_len, d)
    xs = jnp.pad(x_sample, ((0, 0), (0, SAMPLE_ROWS - t_new), (0, 0))).reshape(n_s * SAMPLE_ROWS, d)
    outs = dict(kv_p=[], kv_s=[], h_p=[], h_s=[], conv_p=[], conv_s=[])
    biases = ([_prompt_bias(rel_bias, g, win // dil, dil) for g, (win, dil) in enumerate(ATT_GROUPS)],
              [_sample_bias(rel_bias, g, win, dil, t_new, caches[g].shape[3])
               for g, (win, dil) in enumerate(ATT_GROUPS)])
    for l in range(DEPTH):
        mod = _matmul(c_all, w_mod, l, name="adaln_mod", bias=b_mod, silu_a=True)
        gp = _Group(n_p * s_len, _tile(s_len, TM_TOKEN), mod[:n_p].reshape(n_p, 1, 6 * d), False, s_len)
        gs = _Group(n_s * SAMPLE_ROWS, n_s * SAMPLE_ROWS, jnp.repeat(mod[n_p:n_c], SAMPLE_ROWS, axis=0),
                    True, SAMPLE_ROWS)
        xp, xs, kv_p, kv_s, hl_p, hl_s, conv_p, conv_s = _layer(
            l, xp, xs, gp, gs, p, biases, caches, state_rglru_h, state_conv, dims)
        for key, val in zip(("kv_p", "kv_s", "h_p", "h_s", "conv_p", "conv_s"),
                            (kv_p, kv_s, hl_p, hl_s, conv_p, conv_s)):
            outs[key].append(val)

    def stack_kv(per_layer, g):
        return jnp.stack([r[g] for r in per_layer], axis=1)

    y_prompt = xp.reshape(n_p, s_len, d)
    y_sample = xs.reshape(n_s, SAMPLE_ROWS, d)[:, :t_new]
    return (y_prompt, y_sample,
            stack_kv(outs["kv_p"], 0), stack_kv(outs["kv_p"], 1), stack_kv(outs["kv_p"], 2),
            jnp.stack(outs["h_p"], axis=1), jnp.stack(outs["conv_p"], axis=1),
            stack_kv(outs["kv_s"], 0), stack_kv(outs["kv_s"], 1), stack_kv(outs["kv_s"], 2),
            jnp.stack(outs["h_s"], axis=1), jnp.stack(outs["conv_s"], axis=1))
```
